```python
import math
import jax, jax.numpy as jnp
from jax import lax
import numpy as np

D_MODEL = 4096
BATCH = 2
SEQ = 8192
DEPTH = 4

HEAD_DIM = 128
ROPE_THETA = 10000.0
NORM_EPS = 1e-6
NSA_Q_HEADS = D_MODEL // HEAD_DIM
NSA_KV_GROUPS = 4
NSA_HEADS_PER_GROUP = NSA_Q_HEADS // NSA_KV_GROUPS
NSA_KV_WIDTH = NSA_KV_GROUPS * HEAD_DIM
NSA_BRANCHES = 3
NSA_IN = D_MODEL + NSA_BRANCHES * 2 * NSA_KV_WIDTH + NSA_BRANCHES * NSA_Q_HEADS
CMP_STRIDE = 16
CMP_BLOCK = 2 * CMP_STRIDE
CMP_HIDDEN = 256
SEL_BLOCK = 64
SEL_TOPK = 16
WINDOW = 512
NSA_Q_BLOCK = 64
NEG_INF = -1e30
FORCE_SCORE = 1e30
GDN_HEADS = D_MODEL // HEAD_DIM
GDN_KEY_WIDTH = GDN_HEADS * HEAD_DIM
GDN_IN = 4 * GDN_KEY_WIDTH + 2 * GDN_HEADS
CONV_K = 4
GDN_CHUNK = 64
FFN_HIDDEN = -(-8 * D_MODEL // (3 * 256)) * 256

kernel_name = 'hybrid_nsa_gdn_swiglu'


def rms_norm(x, w):
    xf = x.astype(jnp.float32)
    y = xf * lax.rsqrt(jnp.mean(xf * xf, axis=-1, keepdims=True) + NORM_EPS)
    return (y * w.astype(jnp.float32)).astype(x.dtype)


def rope_tables(pos):
    inv = ROPE_THETA ** (-jnp.arange(0, HEAD_DIM, 2, dtype=jnp.float32) / HEAD_DIM)
    ang = pos[:, None] * inv[None, :]
    return jnp.cos(ang), jnp.sin(ang)


def apply_rope(x, cos, sin):
    half = HEAD_DIM // 2
    shp = (cos.shape[0],) + (1,) * (x.ndim - 3) + (half,)
    c, s = cos.reshape(shp), sin.reshape(shp)
    xf = x.astype(jnp.float32)
    x1, x2 = xf[..., :half], xf[..., half:]
    return jnp.concatenate([x1 * c - x2 * s, x2 * c + x1 * s], axis=-1).astype(x.dtype)


def compress_blocks(tok, pos_emb, w1, w2):
    B, T, G, d = tok.shape
    n_cmp = T // CMP_STRIDE - 1
    chunks = tok.reshape(B, T // CMP_STRIDE, CMP_STRIDE, G, d)
    blocks = jnp.concatenate([chunks[:, :-1], chunks[:, 1:]], axis=2)
    blocks = blocks + pos_emb[None, None, :, None, :].astype(tok.dtype)
    flat = blocks.transpose(0, 1, 3, 2, 4).reshape(B, n_cmp, G, CMP_BLOCK * d)
    return jax.nn.gelu(flat @ w1) @ w2


def nsa_mixer(h, w_in, cmp_pos, cmp_w1, cmp_w2, w_out):
    B, T, _ = h.shape
    dt = h.dtype
    G, Hg, d = NSA_KV_GROUPS, NSA_HEADS_PER_GROUP, HEAD_DIM
    n_cmp = T // CMP_STRIDE - 1
    n_sel = T // SEL_BLOCK
    n_topk = min(SEL_TOPK, n_sel)
    Qb = NSA_Q_BLOCK
    n_qb = T // Qb

    proj = h @ w_in
    q = proj[..., :D_MODEL].reshape(B, T, G, Hg, d)
    kv = proj[..., D_MODEL:D_MODEL + 6 * NSA_KV_WIDTH].reshape(B, T, 3, 2, G, d)
    gates = jax.nn.sigmoid(proj[..., D_MODEL + 6 * NSA_KV_WIDTH:].astype(jnp.float32))
    gates = gates.astype(dt).reshape(B, T, 3, G, Hg)

    cos, sin = rope_tables(jnp.arange(T, dtype=jnp.float32))
    q = apply_rope(q, cos, sin) * jnp.asarray(d ** -0.5, dt)
    k_sel = apply_rope(kv[:, :, 1, 0], cos, sin)
    v_sel = kv[:, :, 1, 1]
    k_win = apply_rope(kv[:, :, 2, 0], cos, sin)
    v_win = kv[:, :, 2, 1]

    cmp_end = jnp.arange(n_cmp, dtype=jnp.int32) * CMP_STRIDE + (CMP_BLOCK - 1)
    cos_c, sin_c = rope_tables(cmp_end.astype(jnp.float32))
    k_cmp = apply_rope(compress_blocks(kv[:, :, 0, 0], cmp_pos[0], cmp_w1[0], cmp_w2[0]), cos_c, sin_c)
    v_cmp = compress_blocks(kv[:, :, 0, 1], cmp_pos[1], cmp_w1[1], cmp_w2[1])

    c_start = np.arange(n_cmp) * CMP_STRIDE
    s_start = np.arange(n_sel) * SEL_BLOCK
    ov = np.minimum(c_start[:, None] + CMP_BLOCK, s_start[None, :] + SEL_BLOCK) - np.maximum(c_start[:, None], s_start[None, :])
    overlap = jnp.asarray(np.clip(ov, 0, None) / CMP_BLOCK, dtype=jnp.float32)

    k_sel_b = k_sel.reshape(B, n_sel, SEL_BLOCK, G, d).transpose(0, 3, 1, 2, 4)
    v_sel_b = v_sel.reshape(B, n_sel, SEL_BLOCK, G, d).transpose(0, 3, 1, 2, 4)
    k_win_p = jnp.pad(k_win, ((0, 0), (WINDOW, 0), (0, 0), (0, 0)))
    v_win_p = jnp.pad(v_win, ((0, 0), (WINDOW, 0), (0, 0), (0, 0)))
    b_ix = jnp.arange(B)[:, None, None, None]
    g_ix = jnp.arange(G)[None, :, None, None]
    blk = jnp.arange(n_sel, dtype=jnp.int32)
    sel_off = jnp.arange(SEL_BLOCK, dtype=jnp.int32)
    win_off = jnp.arange(Qb + WINDOW, dtype=jnp.int32)

    def block_fn(args):
        qb, gb, s0 = args
        t = s0 + jnp.arange(Qb, dtype=jnp.int32)

        sc = jnp.einsum('bqghd,bngd->bqghn', qb, k_cmp, preferred_element_type=jnp.float32)
        mask_c = (cmp_end[None, :] <= t[:, None])[None, :, None, None, :]
        pc = jnp.where(mask_c, jax.nn.softmax(jnp.where(mask_c, sc, NEG_INF), axis=-1), 0.0)
        o_c = jnp.einsum('bqghn,bngd->bqghd', pc.astype(dt), v_cmp)

        imp = jnp.einsum('bqgn,ns->bgqs', pc.sum(axis=3), overlap)
        cur = t // SEL_BLOCK
        forced = (blk[None, :] == 0) | (blk[None, :] == cur[:, None]) | (blk[None, :] == cur[:, None] - 1)
        causal_b = blk[None, :] <= cur[:, None]
        imp = jnp.where(forced[None, None], FORCE_SCORE, jnp.where(causal_b[None, None], imp, NEG_INF))
        _, idx = lax.top_k(imp, n_topk)
        kg = k_sel_b[b_ix, g_ix, idx]
        vg = v_sel_b[b_ix, g_ix, idx]
        tok = idx[..., None] * SEL_BLOCK + sel_off
        mask_s = (tok <= t[None, None, :, None, None]).transpose(0, 2, 1, 3, 4)
        mask_s = mask_s.reshape(B, Qb, G, 1, n_topk * SEL_BLOCK)
        ss = jnp.einsum('bqghd,bgqksd->bqghks', qb, kg, preferred_element_type=jnp.float32)
        ss = ss.reshape(B, Qb, G, Hg, n_topk * SEL_BLOCK)
        ps = jax.nn.softmax(jnp.where(mask_s, ss, NEG_INF), axis=-1)
        ps = ps.reshape(B, Qb, G, Hg, n_topk, SEL_BLOCK).astype(dt)
        o_s = jnp.einsum('bqghks,bgqksd->bqghd', ps, vg)

        kw = lax.dynamic_slice_in_dim(k_win_p, s0, Qb + WINDOW, axis=1)
        vw = lax.dynamic_slice_in_dim(v_win_p, s0, Qb + WINDOW, axis=1)
        kp = s0 - WINDOW + win_off
        mask_w = (kp[None, :] <= t[:, None]) & (kp[None, :] > t[:, None] - WINDOW) & (kp[None, :] >= 0)
        mask_w = mask_w[None, :, None, None, :]
        sw = jnp.einsum('bqghd,bkgd->bqghk', qb, kw, preferred_element_type=jnp.float32)
        pw = jax.nn.softmax(jnp.where(mask_w, sw, NEG_INF), axis=-1).astype(dt)
        o_w = jnp.einsum('bqghk,bkgd->bqghd', pw, vw)

        return gb[:, :, 0, :, :, None] * o_c + gb[:, :, 1, :, :, None] * o_s + gb[:, :, 2, :, :, None] * o_w

    q_blocks = q.reshape(B, n_qb, Qb, G, Hg, d).swapaxes(0, 1)
    gate_blocks = gates.reshape(B, n_qb, Qb, 3, G, Hg).swapaxes(0, 1)
    starts = jnp.arange(n_qb, dtype=jnp.int32) * Qb
    o = lax.map(block_fn, (q_blocks, gate_blocks, starts))
    o = o.swapaxes(0, 1).reshape(B, T, D_MODEL)
    return o @ w_out


def causal_conv(x, w):
    C = x.shape[-1]
    return lax.conv_general_dilated(x, w.astype(x.dtype)[:, None, :], window_strides=(1,),
                                    padding=[(CONV_K - 1, 0)],
                                    dimension_numbers=('NWC', 'WIO', 'NWC'),
                                    feature_group_count=C)


def l2_normalize(x):
    return x * lax.rsqrt(jnp.sum(x * x, axis=-1, keepdims=True) + NORM_EPS)


def chunk_gated_delta_rule(q, k, v, g, beta):
    B, T, H, dk = q.shape
    dv = v.shape[-1]
    C = GDN_CHUNK
    N = T // C
    to_c4 = lambda a: a.reshape(B, N, C, H, a.shape[-1]).transpose(1, 0, 3, 2, 4)
    to_c3 = lambda a: a.reshape(B, N, C, H).transpose(1, 0, 3, 2)
    q, k, v = to_c4(q), to_c4(k), to_c4(v)
    g, beta = to_c3(g), to_c3(beta)
    Gc = jnp.cumsum(g, axis=-1)
    tri = jnp.tril(jnp.ones((C, C), dtype=bool))
    strict = jnp.tril(jnp.ones((C, C), dtype=bool), -1)
    decay = jnp.exp(jnp.where(tri, Gc[..., :, None] - Gc[..., None, :], -jnp.inf))
    kk = jnp.einsum('nbhid,nbhjd->nbhij', k, k)
    A = jnp.where(strict, beta[..., :, None] * kk * decay, 0.0)
    eye = jnp.eye(C, dtype=q.dtype)
    Tinv = lax.linalg.triangular_solve(eye + A, jnp.broadcast_to(eye, A.shape),
                                       left_side=True, lower=True, unit_diagonal=True)
    u = Tinv @ (v * beta[..., None])
    w = Tinv @ (k * (beta * jnp.exp(Gc))[..., None])
    qk = jnp.einsum('nbhid,nbhjd->nbhij', q, k) * decay
    q_dec = q * jnp.exp(Gc)[..., None]
    k_dec = k * jnp.exp(Gc[..., -1:] - Gc)[..., None]
    g_last = jnp.exp(Gc[..., -1])

    def step(S, xs):
        u_c, w_c, qk_c, qd_c, kd_c, gl_c = xs
        v_new = u_c - w_c @ S
        o_c = qd_c @ S + qk_c @ v_new
        S = S * gl_c[..., None, None] + jnp.einsum('bhck,bhcv->bhkv', kd_c, v_new)
        return S, o_c

    S0 = jnp.zeros((B, H, dk, dv), dtype=q.dtype)
    _, o = lax.scan(step, S0, (u, w, qk, q_dec, k_dec, g_last))
    return o.transpose(1, 0, 3, 2, 4).reshape(B, T, H, dv)


def gdn_mixer(h, w_in, conv_w, a_log, dt_bias, norm_w, w_out):
    B, T, _ = h.shape
    dt = h.dtype
    H, d, KW = GDN_HEADS, HEAD_DIM, GDN_KEY_WIDTH
    proj = h @ w_in
    qkv = jax.nn.silu(causal_conv(proj[..., :3 * KW], conv_w)).astype(jnp.float32)
    z = proj[..., 3 * KW:4 * KW].astype(jnp.float32).reshape(B, T, H, d)
    a = proj[..., 4 * KW:4 * KW + H].astype(jnp.float32)
    b = proj[..., 4 * KW + H:4 * KW + 2 * H].astype(jnp.float32)
    q = l2_normalize(qkv[..., :KW].reshape(B, T, H, d)) * (d ** -0.5)
    k = l2_normalize(qkv[..., KW:2 * KW].reshape(B, T, H, d))
    v = qkv[..., 2 * KW:].reshape(B, T, H, d)
    beta = jax.nn.sigmoid(b)
    g = -jnp.exp(a_log.astype(jnp.float32)) * jax.nn.softplus(a + dt_bias.astype(jnp.float32))
    o = chunk_gated_delta_rule(q, k, v, g, beta)
    o = rms_norm(o, norm_w) * jax.nn.silu(z)
    return o.reshape(B, T, KW).astype(dt) @ w_out


def swiglu(h, w_gate, w_up, w_down):
    return (jax.nn.silu(h @ w_gate) * (h @ w_up)) @ w_down


def setup_inputs(seed: int = 0) -> dict:
    key = jax.random.key(seed)
    ks = jax.random.split(key, 20)
    n_nsa = (DEPTH + 1) // 2
    n_gdn = DEPTH // 2
    nrm = lambda k, shape, scale: jax.random.normal(k, shape, jnp.float32) * scale
    x = nrm(ks[0], (BATCH, SEQ, D_MODEL), 1.0)
    norm_mix = 1.0 + nrm(ks[1], (DEPTH, D_MODEL), 0.02)
    norm_ffn = 1.0 + nrm(ks[2], (DEPTH, D_MODEL), 0.02)
    norm_final = 1.0 + nrm(ks[3], (D_MODEL,), 0.02)
    nsa_w_in = nrm(ks[4], (n_nsa, D_MODEL, NSA_IN), D_MODEL ** -0.5)
    nsa_cmp_pos = nrm(ks[5], (n_nsa, 2, CMP_BLOCK, HEAD_DIM), 0.5)
    nsa_cmp_w1 = nrm(ks[6], (n_nsa, 2, CMP_BLOCK * HEAD_DIM, CMP_HIDDEN), (CMP_BLOCK * HEAD_DIM) ** -0.5)
    nsa_cmp_w2 = nrm(ks[7], (n_nsa, 2, CMP_HIDDEN, HEAD_DIM), CMP_HIDDEN ** -0.5)
    nsa_w_out = nrm(ks[8], (n_nsa, D_MODEL, D_MODEL), D_MODEL ** -0.5)
    gdn_w_in = nrm(ks[9], (n_gdn, D_MODEL, GDN_IN), D_MODEL ** -0.5)
    gdn_conv = nrm(ks[10], (n_gdn, CONV_K, 3 * GDN_KEY_WIDTH), CONV_K ** -0.5)
    gdn_a_log = jnp.log(jax.random.uniform(ks[11], (n_gdn, GDN_HEADS), jnp.float32, 1.0, 16.0))
    dt0 = jnp.exp(jax.random.uniform(ks[12], (n_gdn, GDN_HEADS), jnp.float32, math.log(0.001), math.log(0.1)))
    gdn_dt_bias = dt0 + jnp.log(-jnp.expm1(-dt0))
    gdn_norm = 1.0 + nrm(ks[13], (n_gdn, HEAD_DIM), 0.02)
    gdn_w_out = nrm(ks[14], (n_gdn, GDN_KEY_WIDTH, D_MODEL), GDN_KEY_WIDTH ** -0.5)
    ffn_w_gate = nrm(ks[15], (DEPTH, D_MODEL, FFN_HIDDEN), D_MODEL ** -0.5)
    ffn_w_up = nrm(ks[16], (DEPTH, D_MODEL, FFN_HIDDEN), D_MODEL ** -0.5)
    ffn_w_down = nrm(ks[17], (DEPTH, FFN_HIDDEN, D_MODEL), FFN_HIDDEN ** -0.5)
    return {'x': x, 'norm_mix': norm_mix, 'norm_ffn': norm_ffn, 'norm_final': norm_final,
            'nsa_w_in': nsa_w_in, 'nsa_cmp_pos': nsa_cmp_pos, 'nsa_cmp_w1': nsa_cmp_w1,
            'nsa_cmp_w2': nsa_cmp_w2, 'nsa_w_out': nsa_w_out, 'gdn_w_in': gdn_w_in,
            'gdn_conv': gdn_conv, 'gdn_a_log': gdn_a_log, 'gdn_dt_bias': gdn_dt_bias,
            'gdn_norm': gdn_norm, 'gdn_w_out': gdn_w_out, 'ffn_w_gate': ffn_w_gate,
            'ffn_w_up': ffn_w_up, 'ffn_w_down': ffn_w_down}


def reference(x, norm_mix, norm_ffn, norm_final, nsa_w_in, nsa_cmp_pos, nsa_cmp_w1,
              nsa_cmp_w2, nsa_w_out, gdn_w_in, gdn_conv, gdn_a_log, gdn_dt_bias,
              gdn_norm, gdn_w_out, ffn_w_gate, ffn_w_up, ffn_w_down):
    h = x
    for layer in range(DEPTH):
        y = rms_norm(h, norm_mix[layer])
        i = layer // 2
        if layer % 2 == 0:
            y = nsa_mixer(y, nsa_w_in[i], nsa_cmp_pos[i], nsa_cmp_w1[i], nsa_cmp_w2[i], nsa_w_out[i])
        else:
            y = gdn_mixer(y, gdn_w_in[i], gdn_conv[i], gdn_a_log[i], gdn_dt_bias[i], gdn_norm[i], gdn_w_out[i])
        h = h + y
        h = h + swiglu(rms_norm(h, norm_ffn[layer]), ffn_w_gate[layer], ffn_w_up[layer], ffn_w_down[layer])
    return rms_norm(h, norm_final)
```

```python
import functools
import math

import jax
import jax.numpy as jnp
from jax import lax
from jax.experimental import pallas as pl
from jax.experimental.pallas import tpu as pltpu

F32 = jnp.float32
BF16 = jnp.bfloat16

HEAD_DIM = 128
ROPE_THETA = 10000.0
NORM_EPS = 1e-6
NSA_KV_GROUPS = 4
NSA_BRANCHES = 3
CMP_STRIDE = 16
CMP_BLOCK = 2 * CMP_STRIDE
SEL_BLOCK = 64
SEL_TOPK = 16
WINDOW = 512
NEG_INF = -1e30
FORCE_SCORE = 1e30
CONV_K = 4
GDN_CHUNK = 64

V7X_VMEM_LIMIT_BYTES = 56 * 1024 * 1024
LANES = 128
CONV_HALO_ROWS = 8
NSA_Q_TILE = 128
NSA_SEL_KEYS = 256
NSA_WIN_KEYS = 128
GDN_HEADS_PER_STEP = 4


def _cparams(sem):
    return pltpu.CompilerParams(dimension_semantics=sem, vmem_limit_bytes=V7X_VMEM_LIMIT_BYTES)


def _pick(n, prefs):
    for p in prefs:
        if p <= n and n % p == 0:
            return p
    return n


def _dot(a, b):
    return jnp.dot(a, b, preferred_element_type=F32)


def _dot_nt(a, b):
    return lax.dot_general(a, b, (((1,), (1,)), ((), ())), preferred_element_type=F32)


def _split3(x):
    x1 = x.astype(BF16)
    r1 = x - x1.astype(F32)
    x2 = r1.astype(BF16)
    r2 = r1 - x2.astype(F32)
    return x1, x2, r2.astype(BF16)


def _dot_f32_by_exact(x, e):
    x1, x2, x3 = _split3(x)
    return _dot(x1, e) + _dot(x2, e) + _dot(x3, e)


def _dot_exact_by_f32(e, x):
    x1, x2, x3 = _split3(x)
    return _dot(e, x1) + _dot(e, x2) + _dot(e, x3)


def _dot_f32(a, b):
    a1, a2, a3 = _split3(a)
    b1, b2, b3 = _split3(b)
    return (_dot(a1, b1) + (_dot(a1, b2) + _dot(a2, b1))
            + (_dot(a1, b3) + _dot(a2, b2) + _dot(a3, b1)))


def _sigmoid(x):
    return 1.0 / (1.0 + jnp.exp(-x))


def _silu(x):
    return x * _sigmoid(x)


def _rmsnorm_kernel(x_ref, w_ref, o_ref):
    x = x_ref[...]
    ms = jnp.mean(x * x, axis=-1, keepdims=True)
    o_ref[...] = (x * lax.rsqrt(ms + NORM_EPS) * w_ref[...]).astype(o_ref.dtype)


def _rmsnorm(x, w, out_dtype):
    n, d = x.shape
    tm = _pick(n, (256, 128, 64, 32, 16, 8))
    return pl.pallas_call(
        _rmsnorm_kernel,
        grid=(n // tm,),
        in_specs=[pl.BlockSpec((tm, d), lambda i: (i, 0)),
                  pl.BlockSpec((1, d), lambda i: (0, 0))],
        out_specs=pl.BlockSpec((tm, d), lambda i: (i, 0)),
        out_shape=jax.ShapeDtypeStruct((n, d), out_dtype),
        compiler_params=_cparams(("parallel",)),
        name="rmsnorm",
    )(x, w.reshape(1, d))


def _mm_kernel(x_ref, w_ref, o_ref):
    o_ref[...] = _dot(x_ref[...], w_ref[...]).astype(o_ref.dtype)


def _matmul(x, w, out_dtype, name):
    m, k = x.shape
    n = w.shape[1]
    tm = _pick(m, (1024, 512, 256, 128))
    tn = _pick(n, (512, 256, 128))
    return pl.pallas_call(
        _mm_kernel,
        grid=(m // tm, n // tn),
        in_specs=[pl.BlockSpec((tm, k), lambda i, j: (i, 0)),
                  pl.BlockSpec((k, tn), lambda i, j: (0, j))],
        out_specs=pl.BlockSpec((tm, tn), lambda i, j: (i, j)),
        out_shape=jax.ShapeDtypeStruct((m, n), out_dtype),
        compiler_params=_cparams(("parallel", "parallel")),
        name=name,
    )(x, w)


def _mm_res_kernel(x_ref, w_ref, r_ref, o_ref):
    @pl.when(pl.program_id(2) == 0)
    def _():
        o_ref[...] = r_ref[...]

    o_ref[...] += _dot(x_ref[...], w_ref[...])


def _matmul_residual(x, w, res, name):
    m, k = x.shape
    n = w.shape[1]
    tm = _pick(m, (1024, 512, 256, 128))
    tn = _pick(n, (512, 256, 128))
    tk = k if k <= 4096 else _pick(k, (5504, 4096, 2816, 2048, 1408, 1024, 512, 256, 128))
    return pl.pallas_call(
        _mm_res_kernel,
        grid=(m // tm, n // tn, k // tk),
        in_specs=[pl.BlockSpec((tm, tk), lambda i, j, l: (i, l)),
                  pl.BlockSpec((tk, tn), lambda i, j, l: (l, j)),
                  pl.BlockSpec((tm, tn), lambda i, j, l: (i, j))],
        out_specs=pl.BlockSpec((tm, tn), lambda i, j, l: (i, j)),
        out_shape=jax.ShapeDtypeStruct((m, n), F32),
        compiler_params=_cparams(("parallel", "parallel", "arbitrary")),
        name=name,
    )(x, w, res)


def _gateup_kernel(x_ref, wg_ref, wu_ref, o_ref):
    x = x_ref[...]
    g = _dot(x, wg_ref[...])
    u = _dot(x, wu_ref[...])
    o_ref[...] = (_silu(g) * u).astype(o_ref.dtype)


def _gateup(x, wg, wu):
    m, k = x.shape
    n = wg.shape[1]
    tm = _pick(m, (1024, 512, 256, 128))
    tn = _pick(n, (512, 256, 128))
    return pl.pallas_call(
        _gateup_kernel,
        grid=(m // tm, n // tn),
        in_specs=[pl.BlockSpec((tm, k), lambda i, j: (i, 0)),
                  pl.BlockSpec((k, tn), lambda i, j: (0, j)),
                  pl.BlockSpec((k, tn), lambda i, j: (0, j))],
        out_specs=pl.BlockSpec((tm, tn), lambda i, j: (i, j)),
        out_shape=jax.ShapeDtypeStruct((m, n), BF16),
        compiler_params=_cparams(("parallel", "parallel")),
        name="ffn_gateup",
    )(x, wg, wu)


def _rope_tables(pos):
    inv = ROPE_THETA ** (-jnp.arange(0, HEAD_DIM, 2, dtype=F32) / HEAD_DIM)
    ang = pos[:, None] * inv[None, :]
    c, s = jnp.cos(ang), jnp.sin(ang)
    return jnp.concatenate([c, c], axis=-1), jnp.concatenate([-s, s], axis=-1)


def _rope(x, cos2, sin_signed):
    half = HEAD_DIM // 2
    rot = jnp.concatenate([x[:, half:], x[:, :half]], axis=-1)
    return x * cos2 + rot * sin_signed


def _nsa_prep_kernel(p_ref, cos_ref, sin_ref, q_ref, ks_ref, vs_ref, kw_ref, vw_ref,
                     kc_ref, vc_ref, *, n_heads, d_model):
    G, d = NSA_KV_GROUPS, HEAD_DIM
    cos2 = cos_ref[...]
    sin_s = sin_ref[...]
    scale = d ** -0.5
    for h in range(n_heads):
        x = p_ref[0, :, h * d:(h + 1) * d]
        q_ref[0, h] = (_rope(x, cos2, sin_s) * scale).astype(q_ref.dtype)

    def kv_cols(branch, kv, g):
        c0 = d_model + ((branch * 2 + kv) * G + g) * d
        return p_ref[0, :, c0:c0 + d]

    for g in range(G):
        kc_ref[0, g] = kv_cols(0, 0, g)
        vc_ref[0, g] = kv_cols(0, 1, g)
        ks_ref[0, g] = _rope(kv_cols(1, 0, g), cos2, sin_s).astype(ks_ref.dtype)
        vs_ref[0, g] = kv_cols(1, 1, g).astype(vs_ref.dtype)
        kw_ref[0, g] = _rope(kv_cols(2, 0, g), cos2, sin_s).astype(kw_ref.dtype)
        vw_ref[0, g] = kv_cols(2, 1, g).astype(vw_ref.dtype)


def _nsa_prep(proj, cos2, sin_s, n_heads, d_model):
    B, T, W = proj.shape
    G, d = NSA_KV_GROUPS, HEAD_DIM
    tm = _pick(T, (256, 128, 64))
    kv_spec = pl.BlockSpec((1, G, tm, d), lambda b, i: (b, 0, i, 0))
    kv_bf = jax.ShapeDtypeStruct((B, G, T, d), BF16)
    kv_f32 = jax.ShapeDtypeStruct((B, G, T, d), F32)
    return pl.pallas_call(
        functools.partial(_nsa_prep_kernel, n_heads=n_heads, d_model=d_model),
        grid=(B, T // tm),
        in_specs=[pl.BlockSpec((1, tm, W), lambda b, i: (b, i, 0)),
                  pl.BlockSpec((tm, d), lambda b, i: (i, 0)),
                  pl.BlockSpec((tm, d), lambda b, i: (i, 0))],
        out_specs=[pl.BlockSpec((1, n_heads, tm, d), lambda b, i: (b, 0, i, 0)),
                   kv_spec, kv_spec, kv_spec, kv_spec, kv_spec, kv_spec],
        out_shape=[jax.ShapeDtypeStruct((B, n_heads, T, d), BF16),
                   kv_bf, kv_bf, kv_bf, kv_bf, kv_f32, kv_f32],
        compiler_params=_cparams(("parallel", "parallel")),
        name="nsa_prep",
    )(proj, cos2, sin_s)


def _nsa_compress_kernel(ck_ref, cv_ref, pos_ref, w1_ref, w2_ref, cos_ref, sin_ref,
                         ko_ref, vo_ref):
    nc = ck_ref.shape[2]
    half = ck_ref.shape[3]
    for kv, (c_ref, o_ref) in enumerate(((ck_ref, ko_ref), (cv_ref, vo_ref))):
        c = c_ref[0, 0]
        pos = pos_ref[kv]
        a = _dot((c + pos[:, :half]).astype(BF16), w1_ref[kv, :half, :])
        b = _dot((c + pos[:, half:]).astype(BF16), w1_ref[kv, half:, :])
        hid = a + pltpu.roll(b, nc - 1, 0)
        y = _dot(jax.nn.gelu(hid, approximate=True).astype(BF16), w2_ref[kv])
        if kv == 0:
            y = _rope(y, cos_ref[...], sin_ref[...])
        o_ref[0, 0] = y.astype(o_ref.dtype)


def _nsa_compress(kc_in, vc_in, pos, w1, w2, cos_c, sin_c):
    B, G, T, d = kc_in.shape
    nc = T // CMP_STRIDE
    half = CMP_STRIDE * d
    ck = kc_in.reshape(B, G, nc, half)
    cv = vc_in.reshape(B, G, nc, half)
    hidden = w1.shape[-1]
    c_spec = pl.BlockSpec((1, 1, nc, half), lambda b, g: (b, g, 0, 0))
    o_spec = pl.BlockSpec((1, 1, nc, d), lambda b, g: (b, g, 0, 0))
    full = lambda shape: pl.BlockSpec(shape, lambda b, g: (0,) * len(shape))
    return pl.pallas_call(
        _nsa_compress_kernel,
        grid=(B, G),
        in_specs=[c_spec, c_spec, full((2, 1, 2 * half)), full((2, 2 * half, hidden)),
                  full((2, hidden, d)), full((nc, d)), full((nc, d))],
        out_specs=[o_spec, o_spec],
        out_shape=[jax.ShapeDtypeStruct((B, G, nc, d), BF16)] * 2,
        compiler_params=_cparams(("parallel", "parallel")),
        name="nsa_compress",
    )(ck, cv, pos.reshape(2, 1, 2 * half), w1.astype(BF16), w2.astype(BF16), cos_c, sin_c)


def _flash_step(q, k, v, bias, hg, m_scr, l_scr, acc_scr):
    R = q.shape[0]
    C = k.shape[0]
    tq = R // hg
    s = _dot_nt(q, k).reshape(hg, tq, C) + bias[None]
    s = s.reshape(R, C)
    m_prev = m_scr[...]
    m_new = jnp.maximum(m_prev, jnp.max(s, axis=1, keepdims=True))
    alpha = jnp.exp(m_prev - m_new)
    p = jnp.exp(s - pltpu.repeat(m_new, C // LANES, axis=1))
    l_scr[...] = alpha * l_scr[...] + jnp.sum(p, axis=1, keepdims=True)
    acc_scr[...] = alpha * acc_scr[...] + _dot(p.astype(BF16), v)
    m_scr[...] = m_new


def _nsa_attn_kernel(q_ref, kc_ref, vc_ref, ks_ref, vs_ref, kw_ref, vw_ref, gl_ref, ov_ref,
                     o_ref, m_scr, l_scr, acc_scr, *, hg, n_topk):
    tq = q_ref.shape[2]
    d = q_ref.shape[3]
    ncp = kc_ref.shape[2]
    ns = ov_ref.shape[1]
    R = hg * tq
    t0 = pl.program_id(2) * tq
    q = q_ref[0].reshape(R, d)
    t_col = t0 + lax.broadcasted_iota(jnp.int32, (tq, 1), 0)

    cmp_end = lax.broadcasted_iota(jnp.int32, (tq, ncp), 1) * CMP_STRIDE + (CMP_BLOCK - 1)
    mask_c = cmp_end <= t_col
    s = _dot_nt(q, kc_ref[0, 0]).reshape(hg, tq, ncp)
    s = jnp.where(mask_c[None], s, NEG_INF)
    m = jnp.max(s, axis=-1, keepdims=True)
    p = jnp.where(mask_c[None], jnp.exp(s - m), 0.0)
    l = jnp.sum(p, axis=-1, keepdims=True)
    pc = p / jnp.where(l > 0.0, l, 1.0)
    o_c = _dot(pc.reshape(R, ncp).astype(BF16), vc_ref[0, 0])

    imp = _dot_f32_by_exact(jnp.sum(pc, axis=0), ov_ref[...])
    blk = lax.broadcasted_iota(jnp.int32, (tq, ns), 1)
    cur = t_col // SEL_BLOCK
    forced = (blk == 0) | (blk == cur) | (blk == cur - 1)
    causal = blk <= cur
    x = jnp.where(forced, FORCE_SCORE, jnp.where(causal, imp, NEG_INF))
    blk_f = blk.astype(F32)
    sel = jnp.zeros((tq, ns), F32)
    for _ in range(n_topk):
        mx = jnp.max(x, axis=-1, keepdims=True)
        idx = jnp.min(jnp.where(x == mx, blk_f, float(ns)), axis=-1, keepdims=True)
        hit = blk_f == idx
        sel = jnp.where(hit, 1.0, sel)
        x = jnp.where(hit, -jnp.inf, x)
    sel_b = sel.astype(BF16)

    cs = NSA_SEL_KEYS if ks_ref.shape[2] % NSA_SEL_KEYS == 0 else ks_ref.shape[2]
    m_scr[...] = jnp.full(m_scr.shape, NEG_INF, F32)
    l_scr[...] = jnp.zeros(l_scr.shape, F32)
    acc_scr[...] = jnp.zeros(acc_scr.shape, F32)
    blk_row = lax.broadcasted_iota(jnp.int32, (ns, cs), 0)
    key_blk = lax.broadcasted_iota(jnp.int32, (ns, cs), 1) // SEL_BLOCK
    key_off = lax.broadcasted_iota(jnp.int32, (tq, cs), 1)

    def sel_body(c, carry):
        k0 = pl.multiple_of(c * cs, cs)
        expand = jnp.where(blk_row == key_blk + k0 // SEL_BLOCK, 1.0, 0.0).astype(BF16)
        member = _dot(sel_b, expand)
        valid = (member > 0.5) & (key_off + k0 <= t_col)
        bias = jnp.where(valid, 0.0, NEG_INF)
        _flash_step(q, ks_ref[0, 0, pl.ds(k0, cs), :], vs_ref[0, 0, pl.ds(k0, cs), :],
                    bias, hg, m_scr, l_scr, acc_scr)
        return carry

    lax.fori_loop(0, (t0 + tq + cs - 1) // cs, sel_body, 0)
    o_s = acc_scr[...] / l_scr[...]

    cw = NSA_WIN_KEYS
    m_scr[...] = jnp.full(m_scr.shape, NEG_INF, F32)
    l_scr[...] = jnp.zeros(l_scr.shape, F32)
    acc_scr[...] = jnp.zeros(acc_scr.shape, F32)
    win_off = lax.broadcasted_iota(jnp.int32, (tq, cw), 1)

    def win_body(c, carry):
        k0 = pl.multiple_of(c * cw, cw)
        kp = win_off + k0
        valid = (kp <= t_col) & (kp > t_col - WINDOW)
        bias = jnp.where(valid, 0.0, NEG_INF)
        _flash_step(q, kw_ref[0, 0, pl.ds(k0, cw), :], vw_ref[0, 0, pl.ds(k0, cw), :],
                    bias, hg, m_scr, l_scr, acc_scr)
        return carry

    lax.fori_loop(jnp.maximum(t0 - WINDOW, 0) // cw, (t0 + tq + cw - 1) // cw, win_body, 0)
    o_w = acc_scr[...] / l_scr[...]

    gate = _sigmoid(gl_ref[0])
    for h in range(hg):
        rows = slice(h * tq, (h + 1) * tq)
        out = (gate[:, h:h + 1] * o_c[rows]
               + gate[:, hg + h:hg + h + 1] * o_s[rows]
               + gate[:, 2 * hg + h:2 * hg + h + 1] * o_w[rows])
        o_ref[0, :, h * d:(h + 1) * d] = out.astype(o_ref.dtype)


def _nsa_attention(q, k_cmp, v_cmp, k_sel, v_sel, k_win, v_win, gate_logits, overlap):
    B, n_heads, T, d = q.shape
    G = NSA_KV_GROUPS
    hg = n_heads // G
    ncp = k_cmp.shape[2]
    ns = overlap.shape[1]
    tq = _pick(T, (NSA_Q_TILE,))
    R = hg * tq
    kv_spec = pl.BlockSpec((1, 1, T, d), lambda b, g, i: (b, g, 0, 0))
    cmp_spec = pl.BlockSpec((1, 1, ncp, d), lambda b, g, i: (b, g, 0, 0))
    return pl.pallas_call(
        functools.partial(_nsa_attn_kernel, hg=hg, n_topk=min(SEL_TOPK, ns)),
        grid=(B, G, T // tq),
        in_specs=[pl.BlockSpec((1, hg, tq, d), lambda b, g, i: (b, g, i, 0)),
                  cmp_spec, cmp_spec, kv_spec, kv_spec, kv_spec, kv_spec,
                  pl.BlockSpec((1, tq, LANES), lambda b, g, i: (b, i, g)),
                  pl.BlockSpec((ncp, ns), lambda b, g, i: (0, 0))],
        out_specs=pl.BlockSpec((1, tq, hg * d), lambda b, g, i: (b, i, g)),
        out_shape=jax.ShapeDtypeStruct((B, T, n_heads * d), BF16),
        scratch_shapes=[pltpu.VMEM((R, LANES), F32), pltpu.VMEM((R, LANES), F32),
                        pltpu.VMEM((R, d), F32)],
        compiler_params=_cparams(("parallel", "parallel", "parallel")),
        name="nsa_attention",
    )(q, k_cmp, v_cmp, k_sel, v_sel, k_win, v_win, gate_logits, overlap)


def _nsa_mixer(y, h_res, w_in, cmp_pos, cmp_w1, cmp_w2, w_out, B, T):
    D = y.shape[1]
    d, G = HEAD_DIM, NSA_KV_GROUPS
    n_heads = D // d
    hg = n_heads // G
    kv_w = NSA_BRANCHES * 2 * G * d

    proj = _matmul(y, w_in[:, :D + kv_w].astype(BF16), F32, "nsa_in_proj")
    wg = w_in[:, D + kv_w:].reshape(D, NSA_BRANCHES, G, hg).transpose(0, 2, 1, 3)
    wg = jnp.pad(wg.reshape(D, G, NSA_BRANCHES * hg), ((0, 0), (0, 0), (0, LANES - NSA_BRANCHES * hg)))
    gate_logits = _matmul(y, wg.reshape(D, G * LANES).astype(BF16), F32, "nsa_gate_proj")

    cos2, sin_s = _rope_tables(jnp.arange(T, dtype=F32))
    q, k_sel, v_sel, k_win, v_win, kc_in, vc_in = _nsa_prep(
        proj.reshape(B, T, D + kv_w), cos2, sin_s, n_heads, D)

    nc = T // CMP_STRIDE
    cmp_end = jnp.arange(nc, dtype=jnp.int32) * CMP_STRIDE + (CMP_BLOCK - 1)
    cos_c, sin_c = _rope_tables(cmp_end.astype(F32))
    k_cmp, v_cmp = _nsa_compress(kc_in, vc_in, cmp_pos, cmp_w1, cmp_w2, cos_c, sin_c)

    n_sel = T // SEL_BLOCK
    c_start = jnp.arange(nc)[:, None] * CMP_STRIDE
    s_start = jnp.arange(n_sel)[None, :] * SEL_BLOCK
    ov = jnp.minimum(c_start + CMP_BLOCK, s_start + SEL_BLOCK) - jnp.maximum(c_start, s_start)
    overlap = (jnp.clip(ov, 0, None).astype(F32) / CMP_BLOCK).astype(BF16)

    o = _nsa_attention(q, k_cmp, v_cmp, k_sel, v_sel, k_win, v_win,
                       gate_logits.reshape(B, T, G * LANES), overlap)
    return _matmul_residual(o.reshape(B * T, D), w_out.astype(BF16), h_res, "nsa_out_proj")


def _gdn_conv_kernel(x_ref, halo_ref, w_ref, o_ref, xs_scr, *, mode):
    tm = x_ref.shape[1]
    width = x_ref.shape[2]
    hr = CONV_HALO_ROWS
    first = pl.program_id(1) == 0
    xs_scr[0:hr, :] = jnp.where(first, 0.0, halo_ref[0])
    xs_scr[hr:hr + tm, :] = x_ref[0]
    acc = jnp.zeros((tm, width), F32)
    for j in range(CONV_K):
        r0 = hr - (CONV_K - 1) + j
        acc = acc + xs_scr[r0:r0 + tm, :] * w_ref[j:j + 1, :]
    y = _silu(acc)
    if mode == "v":
        o_ref[0] = y
        return
    scale = HEAD_DIM ** -0.5 if mode == "q" else 1.0
    for h in range(width // HEAD_DIM):
        yh = y[:, h * HEAD_DIM:(h + 1) * HEAD_DIM]
        ss = jnp.sum(yh * yh, axis=-1, keepdims=True)
        o_ref[0, :, h * HEAD_DIM:(h + 1) * HEAD_DIM] = yh * lax.rsqrt(ss + NORM_EPS) * scale


def _gdn_conv(proj, conv_w, which, kw):
    B, T, _ = proj.shape
    mode = "qkv"[which]
    tm = _pick(T, (256, 128, 64))
    tc = _pick(kw, (1024, 512, 256, 128))
    ncb = kw // tc
    hr = CONV_HALO_ROWS
    return pl.pallas_call(
        functools.partial(_gdn_conv_kernel, mode=mode),
        grid=(B, T // tm, ncb),
        in_specs=[pl.BlockSpec((1, tm, tc), lambda b, i, c: (b, i, which * ncb + c)),
                  pl.BlockSpec((1, hr, tc),
                               lambda b, i, c: (b, jnp.maximum(i * (tm // hr) - 1, 0), which * ncb + c)),
                  pl.BlockSpec((CONV_K, tc), lambda b, i, c: (0, which * ncb + c))],
        out_specs=pl.BlockSpec((1, tm, tc), lambda b, i, c: (b, i, c)),
        out_shape=jax.ShapeDtypeStruct((B, T, kw), F32),
        scratch_shapes=[pltpu.VMEM((hr + tm, tc), F32)],
        compiler_params=_cparams(("parallel", "parallel", "parallel")),
        name="gdn_conv_" + mode,
    )(proj, proj, conv_w)


def _gdn_gate_kernel(ab_ref, alog_ref, dtb_ref, g_ref, beta_ref, *, n_heads):
    a = ab_ref[:, :n_heads]
    b = ab_ref[:, n_heads:2 * n_heads]
    x = a + dtb_ref[...]
    softplus = jnp.maximum(x, 0.0) + jnp.log(1.0 + jnp.exp(-jnp.abs(x)))
    g_ref[...] = -jnp.exp(alog_ref[...]) * softplus
    beta_ref[...] = _sigmoid(b)


def _gdn_gates(ab, a_log, dt_bias, n_heads):
    n = ab.shape[0]
    tm = _pick(n, (1024, 512, 256, 128))
    row = pl.BlockSpec((tm, n_heads), lambda i: (i, 0))
    vec = pl.BlockSpec((1, n_heads), lambda i: (0, 0))
    return pl.pallas_call(
        functools.partial(_gdn_gate_kernel, n_heads=n_heads),
        grid=(n // tm,),
        in_specs=[pl.BlockSpec((tm, ab.shape[1]), lambda i: (i, 0)), vec, vec],
        out_specs=[row, row],
        out_shape=[jax.ShapeDtypeStruct((n, n_heads), F32)] * 2,
        compiler_params=_cparams(("parallel",)),
        name="gdn_gates",
    )(ab, a_log.reshape(1, n_heads), dt_bias.reshape(1, n_heads))


def _gdn_delta_kernel(q_ref, k_ref, v_ref, z_ref, g_ref, beta_ref, e_ref, nw_ref, o_ref, s_scr,
                      *, hb):
    C = q_ref.shape[1]
    d = HEAD_DIM

    @pl.when(pl.program_id(2) == 0)
    def _():
        s_scr[...] = jnp.zeros(s_scr.shape, F32)

    row = lax.broadcasted_iota(jnp.int32, (C, C), 0)
    col = lax.broadcasted_iota(jnp.int32, (C, C), 1)
    tri = row >= col
    strict = row > col
    eye = row == col
    lower_ones = jnp.where(tri, 1.0, 0.0).astype(BF16)
    eye_f = jnp.where(eye, 1.0, 0.0)

    gc = _dot_exact_by_f32(lower_ones, g_ref[0])
    gc_b = _dot_f32_by_exact(gc, e_ref[0])
    beta_b = _dot_f32_by_exact(beta_ref[0], e_ref[0])

    for h in range(hb):
        cols = slice(h * d, (h + 1) * d)
        q = q_ref[0, :, cols]
        k = k_ref[0, :, cols]
        v = v_ref[0, :, cols]
        gcb = gc_b[:, cols]
        bb = beta_b[:, cols]
        gc_row = jnp.sum(jnp.where(eye, gcb[:, :C], 0.0), axis=0, keepdims=True)
        decay = jnp.exp(jnp.where(tri, gcb[:, :C] - gc_row, -jnp.inf))
        kb = k.astype(BF16)
        a_mat = jnp.where(strict, bb[:, :C] * _dot_nt(kb, kb) * decay, 0.0)

        t_inv = eye_f - a_mat
        pw = _dot_f32(a_mat, a_mat)
        n_sq = max(int(math.ceil(math.log2(C))) - 1, 0)
        for i in range(n_sq):
            t_inv = t_inv + _dot_f32(t_inv, pw)
            if i + 1 < n_sq:
                pw = _dot_f32(pw, pw)

        eg = jnp.exp(gcb)
        gc_last = gcb[C - 1:C, :]
        tb = t_inv.astype(BF16)
        u = _dot(tb, (v * bb).astype(BF16))
        w = _dot(tb, (k * (bb * eg)).astype(BF16))
        qk = _dot_nt(q.astype(BF16), kb) * decay
        q_dec = q * eg
        k_dec = k * jnp.exp(gc_last - gcb)
        s_prev = s_scr[h]
        s_bf = s_prev.astype(BF16)
        v_new = u - _dot(w.astype(BF16), s_bf)
        o = _dot(q_dec.astype(BF16), s_bf) + _dot(qk.astype(BF16), v_new.astype(BF16))
        s_scr[h] = s_prev * jnp.exp(gc_last) + _dot(k_dec.T.astype(BF16), v_new.astype(BF16))

        ms = jnp.mean(o * o, axis=-1, keepdims=True)
        y = o * lax.rsqrt(ms + NORM_EPS) * nw_ref[...]
        o_ref[0, :, cols] = (y * _silu(z_ref[0, :, cols])).astype(o_ref.dtype)


def _gdn_delta(q, k, v, proj, g, beta, norm_w, n_heads):
    B, T, kw = q.shape
    d, C = HEAD_DIM, GDN_CHUNK
    hb = _pick(n_heads, (GDN_HEADS_PER_STEP, 2, 1))
    n_hg = n_heads // hb
    z_blk0 = 3 * n_hg
    expand = (jnp.arange(n_heads)[None, :, None]
              == (jnp.arange(n_hg)[:, None, None] * hb + jnp.arange(hb * d)[None, None, :] // d))
    expand = expand.astype(BF16)
    qkv_spec = pl.BlockSpec((1, C, hb * d), lambda b, hg, c: (b, c, hg))
    gate_spec = pl.BlockSpec((1, C, n_heads), lambda b, hg, c: (b, c, 0))
    return pl.pallas_call(
        functools.partial(_gdn_delta_kernel, hb=hb),
        grid=(B, n_hg, T // C),
        in_specs=[qkv_spec, qkv_spec, qkv_spec,
                  pl.BlockSpec((1, C, hb * d), lambda b, hg, c: (b, c, z_blk0 + hg)),
                  gate_spec, gate_spec,
                  pl.BlockSpec((1, n_heads, hb * d), lambda b, hg, c: (hg, 0, 0)),
                  pl.BlockSpec((1, d), lambda b, hg, c: (0, 0))],
        out_specs=pl.BlockSpec((1, C, hb * d), lambda b, hg, c: (b, c, hg)),
        out_shape=jax.ShapeDtypeStruct((B, T, kw), BF16),
        scratch_shapes=[pltpu.VMEM((hb, d, d), F32)],
        compiler_params=_cparams(("parallel", "parallel", "arbitrary")),
        name="gdn_delta",
    )(q, k, v, proj, g.reshape(B, T, n_heads), beta.reshape(B, T, n_heads), expand,
      norm_w.reshape(1, d))


def _gdn_mixer(y, h_res, w_in, conv_w, a_log, dt_bias, norm_w, w_out, B, T):
    D = y.shape[1]
    n_heads = D // HEAD_DIM
    kw = n_heads * HEAD_DIM
    proj = _matmul(y, w_in[:, :4 * kw].astype(BF16), F32, "gdn_in_proj").reshape(B, T, 4 * kw)
    w_ab = jnp.pad(w_in[:, 4 * kw:], ((0, 0), (0, LANES - 2 * n_heads)))
    ab = _matmul(y, w_ab.astype(BF16), F32, "gdn_ab_proj")
    g, beta = _gdn_gates(ab, a_log, dt_bias, n_heads)
    q = _gdn_conv(proj, conv_w, 0, kw)
    k = _gdn_conv(proj, conv_w, 1, kw)
    v = _gdn_conv(proj, conv_w, 2, kw)
    o = _gdn_delta(q, k, v, proj, g, beta, norm_w, n_heads)
    return _matmul_residual(o.reshape(B * T, kw), w_out.astype(BF16), h_res, "gdn_out_proj")


def kernel(x, norm_mix, norm_ffn, norm_final, nsa_w_in, nsa_cmp_pos, nsa_cmp_w1, nsa_cmp_w2,
           nsa_w_out, gdn_w_in, gdn_conv, gdn_a_log, gdn_dt_bias, gdn_norm, gdn_w_out,
           ffn_w_gate, ffn_w_up, ffn_w_down):
    B, T, D = x.shape
    depth = norm_mix.shape[0]
    h = x.reshape(B * T, D)
    for layer in range(depth):
        y = _rmsnorm(h, norm_mix[layer], BF16)
        i = layer // 2
        if layer % 2 == 0:
            h = _nsa_mixer(y, h, nsa_w_in[i], nsa_cmp_pos[i], nsa_cmp_w1[i], nsa_cmp_w2[i],
                           nsa_w_out[i], B, T)
        else:
            h = _gdn_mixer(y, h, gdn_w_in[i], gdn_conv[i], gdn_a_log[i], gdn_dt_bias[i],
                           gdn_norm[i], gdn_w_out[i], B, T)
        y = _rmsnorm(h, norm_ffn[layer], BF16)
        hid = _gateup(y, ffn_w_gate[layer].astype(BF16), ffn_w_up[layer].astype(BF16))
        h = _matmul_residual(hid, ffn_w_down[layer].astype(BF16), h, "ffn_down")
    return _rmsnorm(h, norm_final, F32).reshape(B, T, D)
```

```python
import functools
import math

import jax
import jax.numpy as jnp
from jax import lax
from jax.experimental import pallas as pl
from jax.experimental.pallas import tpu as pltpu

F32 = jnp.float32
BF16 = jnp.bfloat16

HEAD_DIM = 128
ROPE_THETA = 10000.0
NORM_EPS = 1e-6
NSA_KV_GROUPS = 4
NSA_BRANCHES = 3
CMP_STRIDE = 16
CMP_BLOCK = 2 * CMP_STRIDE
SEL_BLOCK = 64
SEL_TOPK = 16
WINDOW = 512
NEG_INF = -1e30
FORCE_SCORE = 1e30
CONV_K = 4
GDN_CHUNK = 64
LOG2_E = 1.4426950408889634

V7X_VMEM_LIMIT_BYTES = 56 * 1024 * 1024
LANES = 128
CONV_HALO_ROWS = 8
NSA_Q_TILE = 128
NSA_SEL_KEYS = 512
NSA_ROW_SPLITS = 2
GDN_HEADS_PER_STEP = 4
GDN_CHUNKS_PER_STEP = 4


def _cparams(sem):
    return pltpu.CompilerParams(dimension_semantics=sem, vmem_limit_bytes=V7X_VMEM_LIMIT_BYTES)


def _pick(n, prefs):
    for p in prefs:
        if p <= n and n % p == 0:
            return p
    return n


def _dot(a, b):
    return jnp.dot(a, b, preferred_element_type=F32)


def _dot_nt(a, b):
    return lax.dot_general(a, b, (((1,), (1,)), ((), ())), preferred_element_type=F32)


def _split3(x):
    x1 = x.astype(BF16)
    r1 = x - x1.astype(F32)
    x2 = r1.astype(BF16)
    r2 = r1 - x2.astype(F32)
    return x1, x2, r2.astype(BF16)


def _dot_f32_by_exact(x, e):
    x1, x2, x3 = _split3(x)
    return _dot(x1, e) + _dot(x2, e) + _dot(x3, e)


def _dot_exact_by_f32(e, x):
    x1, x2, x3 = _split3(x)
    return _dot(e, x1) + _dot(e, x2) + _dot(e, x3)


def _dot_nt_exact_by_f32(e, x):
    x1, x2, x3 = _split3(x)
    return _dot_nt(e, x1) + _dot_nt(e, x2) + _dot_nt(e, x3)


def _split2(x):
    x1 = x.astype(BF16)
    return x1, (x - x1.astype(F32)).astype(BF16)


def _dot_hi(a, b):
    a1, a2 = _split2(a)
    b1, b2 = _split2(b)
    m = a.shape[0]
    by_b1 = _dot(jnp.concatenate([a1, a2], axis=0), b1)
    return by_b1[:m] + (_dot(a1, b2) + by_b1[m:])


def _sigmoid(x):
    return 1.0 / (1.0 + jnp.exp(-x))


def _silu(x):
    return x * _sigmoid(x)


def _rmsnorm_kernel(x_ref, w_ref, o_ref):
    x = x_ref[...]
    ms = jnp.mean(x * x, axis=-1, keepdims=True)
    o_ref[...] = (x * lax.rsqrt(ms + NORM_EPS) * w_ref[...]).astype(o_ref.dtype)


def _rmsnorm(x, w, out_dtype):
    n, d = x.shape
    tm = _pick(n, (256, 128, 64, 32, 16, 8))
    return pl.pallas_call(
        _rmsnorm_kernel,
        grid=(n // tm,),
        in_specs=[pl.BlockSpec((tm, d), lambda i: (i, 0)),
                  pl.BlockSpec((1, d), lambda i: (0, 0))],
        out_specs=pl.BlockSpec((tm, d), lambda i: (i, 0)),
        out_shape=jax.ShapeDtypeStruct((n, d), out_dtype),
        compiler_params=_cparams(("parallel",)),
        name="rmsnorm",
    )(x, w.reshape(1, d))


def _mm_kernel(x_ref, w_ref, o_ref):
    o_ref[...] = _dot(x_ref[...], w_ref[...]).astype(o_ref.dtype)


def _matmul(x, w, out_dtype, name):
    m, k = x.shape
    n = w.shape[1]
    tm = _pick(m, (1024, 512, 256, 128))
    tn = _pick(n, (512, 256, 128))
    return pl.pallas_call(
        _mm_kernel,
        grid=(m // tm, n // tn),
        in_specs=[pl.BlockSpec((tm, k), lambda i, j: (i, 0)),
                  pl.BlockSpec((k, tn), lambda i, j: (0, j))],
        out_specs=pl.BlockSpec((tm, tn), lambda i, j: (i, j)),
        out_shape=jax.ShapeDtypeStruct((m, n), out_dtype),
        compiler_params=_cparams(("parallel", "parallel")),
        name=name,
    )(x, w)


def _mm_res_kernel(x_ref, w_ref, r_ref, o_ref):
    @pl.when(pl.program_id(2) == 0)
    def _():
        o_ref[...] = r_ref[...]

    o_ref[...] += _dot(x_ref[...], w_ref[...])


def _matmul_residual(x, w, res, name):
    m, k = x.shape
    n = w.shape[1]
    tm = _pick(m, (1024, 512, 256, 128))
    tn = _pick(n, (512, 256, 128))
    tk = k if k <= 4096 else _pick(k, (5504, 4096, 2816, 2048, 1408, 1024, 512, 256, 128))
    return pl.pallas_call(
        _mm_res_kernel,
        grid=(m // tm, n // tn, k // tk),
        in_specs=[pl.BlockSpec((tm, tk), lambda i, j, l: (i, l)),
                  pl.BlockSpec((tk, tn), lambda i, j, l: (l, j)),
                  pl.BlockSpec((tm, tn), lambda i, j, l: (i, j))],
        out_specs=pl.BlockSpec((tm, tn), lambda i, j, l: (i, j)),
        out_shape=jax.ShapeDtypeStruct((m, n), F32),
        compiler_params=_cparams(("parallel", "parallel", "arbitrary")),
        name=name,
    )(x, w, res)


def _gateup_kernel(x_ref, wg_ref, wu_ref, o_ref):
    x = x_ref[...]
    g = _dot(x, wg_ref[...])
    u = _dot(x, wu_ref[...])
    o_ref[...] = (_silu(g) * u).astype(o_ref.dtype)


def _gateup(x, wg, wu):
    m, k = x.shape
    n = wg.shape[1]
    tm = _pick(m, (1024, 512, 256, 128))
    tn = _pick(n, (512, 256, 128))
    return pl.pallas_call(
        _gateup_kernel,
        grid=(m // tm, n // tn),
        in_specs=[pl.BlockSpec((tm, k), lambda i, j: (i, 0)),
                  pl.BlockSpec((k, tn), lambda i, j: (0, j)),
                  pl.BlockSpec((k, tn), lambda i, j: (0, j))],
        out_specs=pl.BlockSpec((tm, tn), lambda i, j: (i, j)),
        out_shape=jax.ShapeDtypeStruct((m, n), BF16),
        compiler_params=_cparams(("parallel", "parallel")),
        name="ffn_gateup",
    )(x, wg, wu)


def _rope_tables(pos):
    inv = ROPE_THETA ** (-jnp.arange(0, HEAD_DIM, 2, dtype=F32) / HEAD_DIM)
    ang = pos[:, None] * inv[None, :]
    c, s = jnp.cos(ang), jnp.sin(ang)
    return jnp.concatenate([c, c], axis=-1), jnp.concatenate([-s, s], axis=-1)


def _rope(x, cos2, sin_signed):
    half = HEAD_DIM // 2
    rot = jnp.concatenate([x[:, half:], x[:, :half]], axis=-1)
    return x * cos2 + rot * sin_signed


def _nsa_prep_kernel(p_ref, cos_ref, sin_ref, q_ref, ks_ref, vs_ref, kw_ref, vw_ref,
                     kc_ref, vc_ref, *, n_heads, d_model):
    G, d = NSA_KV_GROUPS, HEAD_DIM
    cos2 = cos_ref[...]
    sin_s = sin_ref[...]
    scale = d ** -0.5 * LOG2_E
    for h in range(n_heads):
        x = p_ref[0, :, h * d:(h + 1) * d]
        q_ref[0, h] = (_rope(x, cos2, sin_s) * scale).astype(q_ref.dtype)

    def kv_cols(branch, kv, g):
        c0 = d_model + ((branch * 2 + kv) * G + g) * d
        return p_ref[0, :, c0:c0 + d]

    for g in range(G):
        kc_ref[0, g] = kv_cols(0, 0, g)
        vc_ref[0, g] = kv_cols(0, 1, g)
        ks_ref[0, g] = _rope(kv_cols(1, 0, g), cos2, sin_s).astype(ks_ref.dtype)
        vs_ref[0, g] = kv_cols(1, 1, g).astype(vs_ref.dtype)
        kw_ref[0, g] = _rope(kv_cols(2, 0, g), cos2, sin_s).astype(kw_ref.dtype)
        vw_ref[0, g] = kv_cols(2, 1, g).astype(vw_ref.dtype)


def _nsa_prep(proj, cos2, sin_s, n_heads, d_model):
    B, T, W = proj.shape
    G, d = NSA_KV_GROUPS, HEAD_DIM
    tm = _pick(T, (256, 128, 64))
    kv_spec = pl.BlockSpec((1, G, tm, d), lambda b, i: (b, 0, i, 0))
    kv_bf = jax.ShapeDtypeStruct((B, G, T, d), BF16)
    kv_f32 = jax.ShapeDtypeStruct((B, G, T, d), F32)
    return pl.pallas_call(
        functools.partial(_nsa_prep_kernel, n_heads=n_heads, d_model=d_model),
        grid=(B, T // tm),
        in_specs=[pl.BlockSpec((1, tm, W), lambda b, i: (b, i, 0)),
                  pl.BlockSpec((tm, d), lambda b, i: (i, 0)),
                  pl.BlockSpec((tm, d), lambda b, i: (i, 0))],
        out_specs=[pl.BlockSpec((1, n_heads, tm, d), lambda b, i: (b, 0, i, 0)),
                   kv_spec, kv_spec, kv_spec, kv_spec, kv_spec, kv_spec],
        out_shape=[jax.ShapeDtypeStruct((B, n_heads, T, d), BF16),
                   kv_bf, kv_bf, kv_bf, kv_bf, kv_f32, kv_f32],
        compiler_params=_cparams(("parallel", "parallel")),
        name="nsa_prep",
    )(proj, cos2, sin_s)


def _nsa_compress_kernel(ck_ref, cv_ref, pos_ref, w1_ref, w2_ref, cos_ref, sin_ref,
                         ko_ref, vo_ref):
    nc = ck_ref.shape[2]
    half = ck_ref.shape[3]
    for kv, (c_ref, o_ref) in enumerate(((ck_ref, ko_ref), (cv_ref, vo_ref))):
        c = c_ref[0, 0]
        pos = pos_ref[kv]
        a = _dot((c + pos[:, :half]).astype(BF16), w1_ref[kv, :half, :])
        b = _dot((c + pos[:, half:]).astype(BF16), w1_ref[kv, half:, :])
        hid = a + pltpu.roll(b, nc - 1, 0)
        y = _dot(jax.nn.gelu(hid, approximate=True).astype(BF16), w2_ref[kv])
        if kv == 0:
            y = _rope(y, cos_ref[...], sin_ref[...])
        o_ref[0, 0] = y.astype(o_ref.dtype)


def _nsa_compress(kc_in, vc_in, pos, w1, w2, cos_c, sin_c):
    B, G, T, d = kc_in.shape
    nc = T // CMP_STRIDE
    half = CMP_STRIDE * d
    ck = kc_in.reshape(B, G, nc, half)
    cv = vc_in.reshape(B, G, nc, half)
    hidden = w1.shape[-1]
    c_spec = pl.BlockSpec((1, 1, nc, half), lambda b, g: (b, g, 0, 0))
    o_spec = pl.BlockSpec((1, 1, nc, d), lambda b, g: (b, g, 0, 0))
    full = lambda shape: pl.BlockSpec(shape, lambda b, g: (0,) * len(shape))
    return pl.pallas_call(
        _nsa_compress_kernel,
        grid=(B, G),
        in_specs=[c_spec, c_spec, full((2, 1, 2 * half)), full((2, 2 * half, hidden)),
                  full((2, hidden, d)), full((nc, d)), full((nc, d))],
        out_specs=[o_spec, o_spec],
        out_shape=[jax.ShapeDtypeStruct((B, G, nc, d), BF16)] * 2,
        compiler_params=_cparams(("parallel", "parallel")),
        name="nsa_compress",
    )(ck, cv, pos.reshape(2, 1, 2 * half), w1.astype(BF16), w2.astype(BF16), cos_c, sin_c)


def _masked_keys(k, valid_t):
    return jnp.concatenate([k, jnp.where(valid_t, 0.0, NEG_INF).astype(BF16)], axis=1)


def _values_with_ones(v):
    return jnp.concatenate([v, jnp.ones(v.shape, BF16)], axis=1)


def _softmax_attend(q_aug, k_aug, v1):
    d = v1.shape[1] // 2
    s = _dot_nt(q_aug, k_aug)
    p = jnp.exp2((s - jnp.max(s, axis=1, keepdims=True)).astype(BF16))
    pv = _dot(p, v1)
    return pv[:, :d] / pv[:, d:]


def _nsa_attn_kernel(q_ref, kc_ref, vc_ref, ks_ref, vs_ref, kw_ref, vw_ref, gl_ref, ovt_ref,
                     o_ref, m_scr, acc_scr, ow_scr, sa_scr, sb_scr, *, hg, n_topk):
    tq = q_ref.shape[2]
    d = q_ref.shape[3]
    ncp = kc_ref.shape[2]
    ns = ovt_ref.shape[0]
    T = ks_ref.shape[2]
    R = hg * tq
    t0 = pl.program_id(2) * tq
    splits = [slice(i * (R // NSA_ROW_SPLITS), (i + 1) * (R // NSA_ROW_SPLITS))
              for i in range(NSA_ROW_SPLITS)]

    onehot = jnp.where(lax.broadcasted_iota(jnp.int32, (tq, tq), 0)
                       == lax.broadcasted_iota(jnp.int32, (tq, tq), 1), 1.0, 0.0).astype(BF16)
    q_aug = jnp.concatenate([q_ref[0].reshape(R, d), jnp.concatenate([onehot] * hg, axis=0)], axis=1)
    t_row = t0 + lax.broadcasted_iota(jnp.int32, (1, tq), 1)

    wk = min(WINDOW + tq, T)
    w0 = pl.multiple_of(jnp.maximum(t0 - WINDOW, 0), tq)
    kp = w0 + lax.broadcasted_iota(jnp.int32, (wk, tq), 0)
    kw_aug = _masked_keys(kw_ref[0, 0, pl.ds(w0, wk), :], (kp <= t_row) & (kp > t_row - WINDOW))
    vw = _values_with_ones(vw_ref[0, 0, pl.ds(w0, wk), :])
    for rows in splits:
        ow_scr[rows, :] = _softmax_attend(q_aug[rows], kw_aug, vw)

    cmp_end = lax.broadcasted_iota(jnp.int32, (ncp, tq), 0) * CMP_STRIDE + (CMP_BLOCK - 1)
    kc_aug = _masked_keys(kc_ref[0, 0], cmp_end <= t_row)
    s = _dot_nt(q_aug, kc_aug)
    p = jnp.exp2(s - jnp.max(s, axis=1, keepdims=True))
    t_of_row = t0 + lax.broadcasted_iota(jnp.int32, (R, 1), 0) % tq
    inv_l = jnp.where(t_of_row >= CMP_BLOCK - 1, 1.0 / jnp.sum(p, axis=1, keepdims=True), 0.0)
    pc = p * inv_l
    o_c = _dot(pc.astype(BF16), vc_ref[0, 0])

    imp_t = _dot_nt_exact_by_f32(ovt_ref[...], jnp.sum(pc.reshape(hg, tq, ncp), axis=0))
    blk = lax.broadcasted_iota(jnp.int32, (ns, tq), 0)
    cur = t_row // SEL_BLOCK
    forced = (blk == 0) | (blk == cur) | (blk == cur - 1)
    x = jnp.where(forced, FORCE_SCORE, jnp.where(blk <= cur, imp_t, NEG_INF))
    blk_f = blk.astype(F32)
    sel = jnp.zeros((ns, tq), F32)
    for _ in range(n_topk):
        mx = jnp.max(x, axis=0, keepdims=True)
        idx = jnp.min(jnp.where(x == mx, blk_f, float(ns)), axis=0, keepdims=True)
        hit = blk_f == idx
        sel = jnp.where(hit, 1.0, sel)
        x = jnp.where(hit, -jnp.inf, x)
    sel_b = sel.astype(BF16)

    cs = min(NSA_SEL_KEYS, T)
    m_scr[...] = jnp.full(m_scr.shape, NEG_INF, F32)
    acc_scr[...] = jnp.zeros(acc_scr.shape, F32)
    key_blk = lax.broadcasted_iota(jnp.int32, (cs, ns), 0) // SEL_BLOCK
    blk_col = lax.broadcasted_iota(jnp.int32, (cs, ns), 1)
    key_off = lax.broadcasted_iota(jnp.int32, (cs, tq), 0)

    n_chunks = (t0 + tq + cs - 1) // cs
    last_chunk = T // cs - 1

    def scores(c):
        k0 = pl.multiple_of(c * cs, cs)
        expand = jnp.where(key_blk + k0 // SEL_BLOCK == blk_col, 1.0, 0.0).astype(BF16)
        member = _dot(expand, sel_b)
        valid = (member > 0.5) & (key_off + k0 <= t_row)
        return _dot_nt(q_aug, _masked_keys(ks_ref[0, 0, pl.ds(k0, cs), :], valid))

    def attend(c, s_cur, s_next):
        if s_next is not None:
            s_next[...] = scores(jnp.minimum(c + 1, last_chunk))
        v1 = _values_with_ones(vs_ref[0, 0, pl.ds(pl.multiple_of(c * cs, cs), cs), :])
        for rows in splits:
            s = s_cur[rows, :]
            m_prev = m_scr[rows, :]
            m_new = jnp.maximum(m_prev, jnp.max(s, axis=1, keepdims=True))
            alpha = jnp.exp2(m_prev - m_new)
            p = jnp.exp2((s - pltpu.repeat(m_new, cs // LANES, axis=1)).astype(BF16))
            acc_scr[rows, :] = pltpu.repeat(alpha, 2, axis=1) * acc_scr[rows, :] + _dot(p, v1)
            m_scr[rows, :] = m_new

    sa_scr[...] = scores(0)

    def sel_body(j, carry):
        attend(2 * j, sa_scr, sb_scr)
        attend(2 * j + 1, sb_scr, sa_scr)
        return carry

    lax.fori_loop(0, n_chunks // 2, sel_body, 0)

    @pl.when(n_chunks % 2 == 1)
    def _():
        attend(n_chunks - 1, sa_scr, None)

    o_s = acc_scr[:, :d] / acc_scr[:, d:]

    o_w = ow_scr[...]

    gate = _sigmoid(gl_ref[0])
    for h in range(hg):
        rows = slice(h * tq, (h + 1) * tq)
        out = (gate[:, h:h + 1] * o_c[rows]
               + gate[:, hg + h:hg + h + 1] * o_s[rows]
               + gate[:, 2 * hg + h:2 * hg + h + 1] * o_w[rows])
        o_ref[0, :, h * d:(h + 1) * d] = out.astype(o_ref.dtype)


def _nsa_attention(q, k_cmp, v_cmp, k_sel, v_sel, k_win, v_win, gate_logits, overlap_t):
    B, n_heads, T, d = q.shape
    G = NSA_KV_GROUPS
    hg = n_heads // G
    ncp = k_cmp.shape[2]
    ns = overlap_t.shape[0]
    tq = _pick(T, (NSA_Q_TILE,))
    R = hg * tq
    kv_spec = pl.BlockSpec((1, 1, T, d), lambda b, g, i: (b, g, 0, 0))
    cmp_spec = pl.BlockSpec((1, 1, ncp, d), lambda b, g, i: (b, g, 0, 0))
    return pl.pallas_call(
        functools.partial(_nsa_attn_kernel, hg=hg, n_topk=min(SEL_TOPK, ns)),
        grid=(B, G, T // tq),
        in_specs=[pl.BlockSpec((1, hg, tq, d), lambda b, g, i: (b, g, i, 0)),
                  cmp_spec, cmp_spec, kv_spec, kv_spec, kv_spec, kv_spec,
                  pl.BlockSpec((1, tq, LANES), lambda b, g, i: (b, i, g)),
                  pl.BlockSpec((ns, ncp), lambda b, g, i: (0, 0))],
        out_specs=pl.BlockSpec((1, tq, hg * d), lambda b, g, i: (b, i, g)),
        out_shape=jax.ShapeDtypeStruct((B, T, n_heads * d), BF16),
        scratch_shapes=[pltpu.VMEM((R, LANES), F32), pltpu.VMEM((R, 2 * d), F32),
                        pltpu.VMEM((R, d), F32),
                        pltpu.VMEM((R, min(NSA_SEL_KEYS, T)), F32),
                        pltpu.VMEM((R, min(NSA_SEL_KEYS, T)), F32)],
        compiler_params=_cparams(("parallel", "parallel", "parallel")),
        name="nsa_attention",
    )(q, k_cmp, v_cmp, k_sel, v_sel, k_win, v_win, gate_logits, overlap_t)


def _nsa_mixer(y, h_res, w_in, cmp_pos, cmp_w1, cmp_w2, w_out, B, T):
    D = y.shape[1]
    d, G = HEAD_DIM, NSA_KV_GROUPS
    n_heads = D // d
    hg = n_heads // G
    kv_w = NSA_BRANCHES * 2 * G * d

    proj = _matmul(y, w_in[:, :D + kv_w].astype(BF16), F32, "nsa_in_proj")
    wg = w_in[:, D + kv_w:].reshape(D, NSA_BRANCHES, G, hg).transpose(0, 2, 1, 3)
    wg = jnp.pad(wg.reshape(D, G, NSA_BRANCHES * hg), ((0, 0), (0, 0), (0, LANES - NSA_BRANCHES * hg)))
    gate_logits = _matmul(y, wg.reshape(D, G * LANES).astype(BF16), F32, "nsa_gate_proj")

    cos2, sin_s = _rope_tables(jnp.arange(T, dtype=F32))
    q, k_sel, v_sel, k_win, v_win, kc_in, vc_in = _nsa_prep(
        proj.reshape(B, T, D + kv_w), cos2, sin_s, n_heads, D)

    nc = T // CMP_STRIDE
    cmp_end = jnp.arange(nc, dtype=jnp.int32) * CMP_STRIDE + (CMP_BLOCK - 1)
    cos_c, sin_c = _rope_tables(cmp_end.astype(F32))
    k_cmp, v_cmp = _nsa_compress(kc_in, vc_in, cmp_pos, cmp_w1, cmp_w2, cos_c, sin_c)

    n_sel = T // SEL_BLOCK
    c_start = jnp.arange(nc)[None, :] * CMP_STRIDE
    s_start = jnp.arange(n_sel)[:, None] * SEL_BLOCK
    ov = jnp.minimum(c_start + CMP_BLOCK, s_start + SEL_BLOCK) - jnp.maximum(c_start, s_start)
    overlap_t = (jnp.clip(ov, 0, None).astype(F32) / CMP_BLOCK).astype(BF16)

    o = _nsa_attention(q, k_cmp, v_cmp, k_sel, v_sel, k_win, v_win,
                       gate_logits.reshape(B, T, G * LANES), overlap_t)
    return _matmul_residual(o.reshape(B * T, D), w_out.astype(BF16), h_res, "nsa_out_proj")


def _gdn_conv_kernel(x_ref, halo_ref, w_ref, o_ref, xs_scr, *, mode):
    tm = x_ref.shape[1]
    width = x_ref.shape[2]
    hr = CONV_HALO_ROWS
    first = pl.program_id(1) == 0
    xs_scr[0:hr, :] = jnp.where(first, 0.0, halo_ref[0])
    xs_scr[hr:hr + tm, :] = x_ref[0]
    acc = jnp.zeros((tm, width), F32)
    for j in range(CONV_K):
        r0 = hr - (CONV_K - 1) + j
        acc = acc + xs_scr[r0:r0 + tm, :] * w_ref[j:j + 1, :]
    y = _silu(acc)
    if mode == "v":
        o_ref[0] = y
        return
    scale = HEAD_DIM ** -0.5 if mode == "q" else 1.0
    for h in range(width // HEAD_DIM):
        yh = y[:, h * HEAD_DIM:(h + 1) * HEAD_DIM]
        ss = jnp.sum(yh * yh, axis=-1, keepdims=True)
        o_ref[0, :, h * HEAD_DIM:(h + 1) * HEAD_DIM] = yh * lax.rsqrt(ss + NORM_EPS) * scale


def _gdn_conv(proj, conv_w, which, kw):
    B, T, _ = proj.shape
    mode = "qkv"[which]
    tm = _pick(T, (256, 128, 64))
    tc = _pick(kw, (1024, 512, 256, 128))
    ncb = kw // tc
    hr = CONV_HALO_ROWS
    return pl.pallas_call(
        functools.partial(_gdn_conv_kernel, mode=mode),
        grid=(B, T // tm, ncb),
        in_specs=[pl.BlockSpec((1, tm, tc), lambda b, i, c: (b, i, which * ncb + c)),
                  pl.BlockSpec((1, hr, tc),
                               lambda b, i, c: (b, jnp.maximum(i * (tm // hr) - 1, 0), which * ncb + c)),
                  pl.BlockSpec((CONV_K, tc), lambda b, i, c: (0, which * ncb + c))],
        out_specs=pl.BlockSpec((1, tm, tc), lambda b, i, c: (b, i, c)),
        out_shape=jax.ShapeDtypeStruct((B, T, kw), F32),
        scratch_shapes=[pltpu.VMEM((hr + tm, tc), F32)],
        compiler_params=_cparams(("parallel", "parallel", "parallel")),
        name="gdn_conv_" + mode,
    )(proj, proj, conv_w)


def _gdn_gate_kernel(ab_ref, alog_ref, dtb_ref, g_ref, beta_ref, *, n_heads):
    a = ab_ref[:, :n_heads]
    b = ab_ref[:, n_heads:2 * n_heads]
    x = a + dtb_ref[...]
    softplus = jnp.maximum(x, 0.0) + jnp.log(1.0 + jnp.exp(-jnp.abs(x)))
    g_ref[...] = -jnp.exp(alog_ref[...]) * softplus
    beta_ref[...] = _sigmoid(b)


def _gdn_gates(ab, a_log, dt_bias, n_heads):
    n = ab.shape[0]
    tm = _pick(n, (1024, 512, 256, 128))
    row = pl.BlockSpec((tm, n_heads), lambda i: (i, 0))
    vec = pl.BlockSpec((1, n_heads), lambda i: (0, 0))
    return pl.pallas_call(
        functools.partial(_gdn_gate_kernel, n_heads=n_heads),
        grid=(n // tm,),
        in_specs=[pl.BlockSpec((tm, ab.shape[1]), lambda i: (i, 0)), vec, vec],
        out_specs=[row, row],
        out_shape=[jax.ShapeDtypeStruct((n, n_heads), F32)] * 2,
        compiler_params=_cparams(("parallel",)),
        name="gdn_gates",
    )(ab, a_log.reshape(1, n_heads), dt_bias.reshape(1, n_heads))


def _gdn_delta_kernel(q_ref, k_ref, v_ref, z_ref, g_ref, beta_ref, e_ref, nw_ref, o_ref, s_scr,
                      *, hb, nc):
    C, d = GDN_CHUNK, HEAD_DIM
    rows_all = nc * C

    @pl.when(pl.program_id(2) == 0)
    def _():
        s_scr[...] = jnp.zeros(s_scr.shape, F32)

    row = lax.broadcasted_iota(jnp.int32, (C, C), 0)
    col = lax.broadcasted_iota(jnp.int32, (C, C), 1)
    tri = row >= col
    strict = row > col
    eye = row == col
    eye_f = jnp.where(eye, 1.0, 0.0)
    r2 = lax.broadcasted_iota(jnp.int32, (rows_all, rows_all), 0)
    c2 = lax.broadcasted_iota(jnp.int32, (rows_all, rows_all), 1)
    chunk_lower = jnp.where((r2 >= c2) & (r2 // C == c2 // C), 1.0, 0.0).astype(BF16)

    gc = _dot_exact_by_f32(chunk_lower, g_ref[0])
    gc_b = _dot_f32_by_exact(gc, e_ref[0])
    beta_b = _dot_f32_by_exact(beta_ref[0], e_ref[0])

    pairs = [(c, h) for c in range(nc) for h in range(hb)]
    loc = []
    for c, h in pairs:
        rs, cols = slice(c * C, (c + 1) * C), slice(h * d, (h + 1) * d)
        q, k, v = q_ref[0, rs, cols], k_ref[0, rs, cols], v_ref[0, rs, cols]
        gcb, bb = gc_b[rs, cols], beta_b[rs, cols]
        gc_row = jnp.sum(jnp.where(eye, gcb[:, :C], 0.0), axis=0, keepdims=True)
        decay = jnp.exp(jnp.where(tri, gcb[:, :C] - gc_row, -jnp.inf))
        kb = k.astype(BF16)
        a_mat = jnp.where(strict, bb[:, :C] * _dot_nt(kb, kb) * decay, 0.0)
        loc.append(dict(q=q, k=k, v=v, gcb=gcb, bb=bb, decay=decay, kb=kb, a=a_mat))

    t_inv = [eye_f - p["a"] for p in loc]
    pw = [_dot_hi(p["a"], p["a"]) for p in loc]
    n_sq = max(int(math.ceil(math.log2(C))) - 1, 0)
    for i in range(n_sq):
        t_inv = [t + _dot_hi(t, p) for t, p in zip(t_inv, pw)]
        if i + 1 < n_sq:
            pw = [_dot_hi(p, p) for p in pw]

    uw = []
    for p, t in zip(loc, t_inv):
        p["eg"] = jnp.exp(p["gcb"])
        rhs = jnp.concatenate([p["v"] * p["bb"], p["k"] * (p["bb"] * p["eg"])], axis=1)
        uw.append(_dot(t.astype(BF16), rhs.astype(BF16)))
    wu_b = [jnp.concatenate([x[:, d:], x[:, :d]], axis=1).astype(BF16) for x in uw]
    qk_wu = [_dot((_dot_nt(p["q"].astype(BF16), p["kb"]) * p["decay"]).astype(BF16), wu)
             for p, wu in zip(loc, wu_b)]
    kd_wu = []
    for p, wu in zip(loc, wu_b):
        p["gc_last"] = p["gcb"][C - 1:C, :]
        k_dec = p["k"] * jnp.exp(p["gc_last"] - p["gcb"])
        kd_wu.append(_dot(k_dec.T.astype(BF16), wu))

    states = [s_scr[h] for h in range(hb)]
    for c in range(nc):
        idx = [c * hb + h for h in range(hb)]
        lhs = [jnp.concatenate([kd_wu[i][:, :d], loc[i]["q"] * loc[i]["eg"] - qk_wu[i][:, :d]],
                               axis=0).astype(BF16) for i in idx]
        prod = [_dot(l, s.astype(BF16)) for l, s in zip(lhs, states)]
        for h, i in enumerate(idx):
            o = prod[h][d:] + qk_wu[i][:, d:]
            states[h] = states[h] * jnp.exp(loc[i]["gc_last"]) - prod[h][:d] + kd_wu[i][:, d:]
            ms = jnp.mean(o * o, axis=-1, keepdims=True)
            y = o * lax.rsqrt(ms + NORM_EPS) * nw_ref[...]
            rs, cols = slice(c * C, (c + 1) * C), slice(h * d, (h + 1) * d)
            o_ref[0, rs, cols] = (y * _silu(z_ref[0, rs, cols])).astype(o_ref.dtype)
    for h in range(hb):
        s_scr[h] = states[h]


def _gdn_delta(q, k, v, proj, g, beta, norm_w, n_heads):
    B, T, kw = q.shape
    d, C = HEAD_DIM, GDN_CHUNK
    hb = _pick(n_heads, (GDN_HEADS_PER_STEP, 2, 1))
    nc = _pick(T // C, (GDN_CHUNKS_PER_STEP, 2, 1))
    n_hg = n_heads // hb
    z_blk0 = 3 * n_hg
    expand = (jnp.arange(n_heads)[None, :, None]
              == (jnp.arange(n_hg)[:, None, None] * hb + jnp.arange(hb * d)[None, None, :] // d))
    expand = expand.astype(BF16)
    qkv_spec = pl.BlockSpec((1, nc * C, hb * d), lambda b, hg, c: (b, c, hg))
    gate_spec = pl.BlockSpec((1, nc * C, n_heads), lambda b, hg, c: (b, c, 0))
    return pl.pallas_call(
        functools.partial(_gdn_delta_kernel, hb=hb, nc=nc),
        grid=(B, n_hg, T // (nc * C)),
        in_specs=[qkv_spec, qkv_spec, qkv_spec,
                  pl.BlockSpec((1, nc * C, hb * d), lambda b, hg, c: (b, c, z_blk0 + hg)),
                  gate_spec, gate_spec,
                  pl.BlockSpec((1, n_heads, hb * d), lambda b, hg, c: (hg, 0, 0)),
                  pl.BlockSpec((1, d), lambda b, hg, c: (0, 0))],
        out_specs=pl.BlockSpec((1, nc * C, hb * d), lambda b, hg, c: (b, c, hg)),
        out_shape=jax.ShapeDtypeStruct((B, T, kw), BF16),
        scratch_shapes=[pltpu.VMEM((hb, d, d), F32)],
        compiler_params=_cparams(("parallel", "parallel", "arbitrary")),
        name="gdn_delta",
    )(q, k, v, proj, g.reshape(B, T, n_heads), beta.reshape(B, T, n_heads), expand,
      norm_w.reshape(1, d))


def _gdn_mixer(y, h_res, w_in, conv_w, a_log, dt_bias, norm_w, w_out, B, T):
    D = y.shape[1]
    n_heads = D // HEAD_DIM
    kw = n_heads * HEAD_DIM
    proj = _matmul(y, w_in[:, :4 * kw].astype(BF16), F32, "gdn_in_proj").reshape(B, T, 4 * kw)
    w_ab = jnp.pad(w_in[:, 4 * kw:], ((0, 0), (0, LANES - 2 * n_heads)))
    ab = _matmul(y, w_ab.astype(BF16), F32, "gdn_ab_proj")
    g, beta = _gdn_gates(ab, a_log, dt_bias, n_heads)
    q = _gdn_conv(proj, conv_w, 0, kw)
    k = _gdn_conv(proj, conv_w, 1, kw)
    v = _gdn_conv(proj, conv_w, 2, kw)
    o = _gdn_delta(q, k, v, proj, g, beta, norm_w, n_heads)
    return _matmul_residual(o.reshape(B * T, kw), w_out.astype(BF16), h_res, "gdn_out_proj")


def kernel(x, norm_mix, norm_ffn, norm_final, nsa_w_in, nsa_cmp_pos, nsa_cmp_w1, nsa_cmp_w2,
           nsa_w_out, gdn_w_in, gdn_conv, gdn_a_log, gdn_dt_bias, gdn_norm, gdn_w_out,
           ffn_w_gate, ffn_w_up, ffn_w_down):
    B, T, D = x.shape
    depth = norm_mix.shape[0]
    h = x.reshape(B * T, D)
    for layer in range(depth):
        y = _rmsnorm(h, norm_mix[layer], BF16)
        i = layer // 2
        if layer % 2 == 0:
            h = _nsa_mixer(y, h, nsa_w_in[i], nsa_cmp_pos[i], nsa_cmp_w1[i], nsa_cmp_w2[i],
                           nsa_w_out[i], B, T)
        else:
            h = _gdn_mixer(y, h, gdn_w_in[i], gdn_conv[i], gdn_a_log[i], gdn_dt_bias[i],
                           gdn_norm[i], gdn_w_out[i], B, T)
        y = _rmsnorm(h, norm_ffn[layer], BF16)
        hid = _gateup(y, ffn_w_gate[layer].astype(BF16), ffn_w_up[layer].astype(BF16))
        h = _matmul_residual(hid, ffn_w_down[layer].astype(BF16), h, "ffn_down")
    return _rmsnorm(h, norm_final, F32).reshape(B, T, D)
```

```python
import functools
import math

import jax
import jax.numpy as jnp
from jax import lax
from jax.experimental import pallas as pl
from jax.experimental.pallas import tpu as pltpu

F32 = jnp.float32
BF16 = jnp.bfloat16

HEAD_DIM = 128
ROPE_THETA = 10000.0
NORM_EPS = 1e-6
NSA_KV_GROUPS = 4
NSA_BRANCHES = 3
CMP_STRIDE = 16
CMP_BLOCK = 2 * CMP_STRIDE
SEL_BLOCK = 64
SEL_TOPK = 16
SEL_FORCED = 3
WINDOW = 512
NEG_INF = -1e30
FORCE_SCORE = 1e30
CONV_K = 4
GDN_CHUNK = 64
LOG2_E = 1.4426950408889634

V7X_VMEM_LIMIT_BYTES = 56 * 1024 * 1024
LANES = 128
CONV_HALO_ROWS = 8
NSA_Q_TILE = 128
NSA_SEL_KEYS = 512
GDN_HEADS_PER_STEP = 8
GDN_CHUNKS_PER_STEP = 2


def _cparams(sem):
    return pltpu.CompilerParams(dimension_semantics=sem, vmem_limit_bytes=V7X_VMEM_LIMIT_BYTES)


def _pick(n, prefs):
    for p in prefs:
        if p <= n and n % p == 0:
            return p
    return n


def _dot(a, b):
    return jnp.dot(a, b, preferred_element_type=F32)


def _dot_nt(a, b):
    return lax.dot_general(a, b, (((1,), (1,)), ((), ())), preferred_element_type=F32)


def _split3(x):
    x1 = x.astype(BF16)
    r1 = x - x1.astype(F32)
    x2 = r1.astype(BF16)
    r2 = r1 - x2.astype(F32)
    return x1, x2, r2.astype(BF16)


def _dot_f32_by_exact(x, e):
    x1, x2, x3 = _split3(x)
    return _dot(x1, e) + _dot(x2, e) + _dot(x3, e)


def _dot_exact_by_f32(e, x):
    x1, x2, x3 = _split3(x)
    return _dot(e, x1) + _dot(e, x2) + _dot(e, x3)


def _dot_nt_exact_by_f32(e, x):
    x1, x2, x3 = _split3(x)
    return _dot_nt(e, x1) + _dot_nt(e, x2) + _dot_nt(e, x3)


def _dot_bf(a, b):
    return _dot(a.astype(BF16), b.astype(BF16))


def _sigmoid(x):
    return 1.0 / (1.0 + jnp.exp(-x))


def _silu(x):
    return x * _sigmoid(x)


def _rmsnorm_kernel(x_ref, w_ref, o_ref):
    x = x_ref[...]
    ms = jnp.mean(x * x, axis=-1, keepdims=True)
    o_ref[...] = (x * lax.rsqrt(ms + NORM_EPS) * w_ref[...]).astype(o_ref.dtype)


def _rmsnorm(x, w, out_dtype):
    n, d = x.shape
    tm = _pick(n, (256, 128, 64, 32, 16, 8))
    return pl.pallas_call(
        _rmsnorm_kernel,
        grid=(n // tm,),
        in_specs=[pl.BlockSpec((tm, d), lambda i: (i, 0)),
                  pl.BlockSpec((1, d), lambda i: (0, 0))],
        out_specs=pl.BlockSpec((tm, d), lambda i: (i, 0)),
        out_shape=jax.ShapeDtypeStruct((n, d), out_dtype),
        compiler_params=_cparams(("parallel",)),
        name="rmsnorm",
    )(x, w.reshape(1, d))


def _mm_kernel(x_ref, w_ref, o_ref):
    o_ref[...] = _dot(x_ref[...], w_ref[...]).astype(o_ref.dtype)


def _matmul(x, w, out_dtype, name, n=None):
    m, k = x.shape
    n = w.shape[1] if n is None else n
    tm = _pick(m, (1024, 512, 256, 128))
    tn = _pick(n, (512, 256, 128))
    return pl.pallas_call(
        _mm_kernel,
        grid=(m // tm, n // tn),
        in_specs=[pl.BlockSpec((tm, k), lambda i, j: (i, 0)),
                  pl.BlockSpec((k, tn), lambda i, j: (0, j))],
        out_specs=pl.BlockSpec((tm, tn), lambda i, j: (i, j)),
        out_shape=jax.ShapeDtypeStruct((m, n), out_dtype),
        compiler_params=_cparams(("parallel", "parallel")),
        name=name,
    )(x, w)


def _mm_res_kernel(x_ref, w_ref, r_ref, o_ref):
    @pl.when(pl.program_id(2) == 0)
    def _():
        o_ref[...] = r_ref[...]

    o_ref[...] += _dot(x_ref[...], w_ref[...])


def _matmul_residual(x, w, res, name):
    m, k = x.shape
    n = w.shape[1]
    tm = _pick(m, (1024, 512, 256, 128))
    tn = _pick(n, (512, 256, 128))
    tk = k if k <= 4096 else _pick(k, (5504, 4096, 2816, 2048, 1408, 1024, 512, 256, 128))
    return pl.pallas_call(
        _mm_res_kernel,
        grid=(m // tm, n // tn, k // tk),
        in_specs=[pl.BlockSpec((tm, tk), lambda i, j, l: (i, l)),
                  pl.BlockSpec((tk, tn), lambda i, j, l: (l, j)),
                  pl.BlockSpec((tm, tn), lambda i, j, l: (i, j))],
        out_specs=pl.BlockSpec((tm, tn), lambda i, j, l: (i, j)),
        out_shape=jax.ShapeDtypeStruct((m, n), F32),
        compiler_params=_cparams(("parallel", "parallel", "arbitrary")),
        name=name,
    )(x, w, res)


def _gateup_kernel(x_ref, wg_ref, wu_ref, o_ref):
    x = x_ref[...]
    g = _dot(x, wg_ref[...])
    u = _dot(x, wu_ref[...])
    o_ref[...] = (_silu(g) * u).astype(o_ref.dtype)


def _gateup(x, wg, wu):
    m, k = x.shape
    n = wg.shape[1]
    tm = _pick(m, (1024, 512, 256, 128))
    tn = _pick(n, (512, 256, 128))
    return pl.pallas_call(
        _gateup_kernel,
        grid=(m // tm, n // tn),
        in_specs=[pl.BlockSpec((tm, k), lambda i, j: (i, 0)),
                  pl.BlockSpec((k, tn), lambda i, j: (0, j)),
                  pl.BlockSpec((k, tn), lambda i, j: (0, j))],
        out_specs=pl.BlockSpec((tm, tn), lambda i, j: (i, j)),
        out_shape=jax.ShapeDtypeStruct((m, n), BF16),
        compiler_params=_cparams(("parallel", "parallel")),
        name="ffn_gateup",
    )(x, wg, wu)


def _rope_tables(pos):
    inv = ROPE_THETA ** (-jnp.arange(0, HEAD_DIM, 2, dtype=F32) / HEAD_DIM)
    ang = pos[:, None] * inv[None, :]
    c, s = jnp.cos(ang), jnp.sin(ang)
    return jnp.concatenate([c, c], axis=-1), jnp.concatenate([-s, s], axis=-1)


def _rope(x, cos2, sin_signed):
    half = HEAD_DIM // 2
    rot = jnp.concatenate([x[:, half:], x[:, :half]], axis=-1)
    return x * cos2 + rot * sin_signed


def _nsa_prep_kernel(p_ref, cos_ref, sin_ref, q_ref, ks_ref, vs_ref, kw_ref, vw_ref,
                     kc_ref, vc_ref, *, n_heads, d_model):
    G, d = NSA_KV_GROUPS, HEAD_DIM
    cos2 = cos_ref[...]
    sin_s = sin_ref[...]
    scale = d ** -0.5 * LOG2_E
    for h in range(n_heads):
        x = p_ref[0, :, h * d:(h + 1) * d]
        q_ref[0, h] = (_rope(x, cos2, sin_s) * scale).astype(q_ref.dtype)

    def kv_cols(branch, kv, g):
        c0 = d_model + ((branch * 2 + kv) * G + g) * d
        return p_ref[0, :, c0:c0 + d]

    for g in range(G):
        kc_ref[0, g] = kv_cols(0, 0, g)
        vc_ref[0, g] = kv_cols(0, 1, g)
        ks_ref[0, g] = _rope(kv_cols(1, 0, g), cos2, sin_s).astype(ks_ref.dtype)
        vs_ref[0, g] = kv_cols(1, 1, g).astype(vs_ref.dtype)
        kw_ref[0, g] = _rope(kv_cols(2, 0, g), cos2, sin_s).astype(kw_ref.dtype)
        vw_ref[0, g] = kv_cols(2, 1, g).astype(vw_ref.dtype)


def _nsa_prep(proj, cos2, sin_s, n_heads, d_model):
    B, T, W = proj.shape
    G, d = NSA_KV_GROUPS, HEAD_DIM
    tm = _pick(T, (256, 128, 64))
    kv_spec = pl.BlockSpec((1, G, tm, d), lambda b, i: (b, 0, i, 0))
    kv_bf = jax.ShapeDtypeStruct((B, G, T, d), BF16)
    kv_f32 = jax.ShapeDtypeStruct((B, G, T, d), F32)
    return pl.pallas_call(
        functools.partial(_nsa_prep_kernel, n_heads=n_heads, d_model=d_model),
        grid=(B, T // tm),
        in_specs=[pl.BlockSpec((1, tm, W), lambda b, i: (b, i, 0)),
                  pl.BlockSpec((tm, d), lambda b, i: (i, 0)),
                  pl.BlockSpec((tm, d), lambda b, i: (i, 0))],
        out_specs=[pl.BlockSpec((1, n_heads, tm, d), lambda b, i: (b, 0, i, 0)),
                   kv_spec, kv_spec, kv_spec, kv_spec, kv_spec, kv_spec],
        out_shape=[jax.ShapeDtypeStruct((B, n_heads, T, d), BF16),
                   kv_bf, kv_bf, kv_bf, kv_bf, kv_f32, kv_f32],
        compiler_params=_cparams(("parallel", "parallel")),
        name="nsa_prep",
    )(proj, cos2, sin_s)


def _nsa_compress_kernel(ck_ref, cv_ref, pos_ref, w1_ref, w2_ref, cos_ref, sin_ref,
                         ko_ref, vo_ref):
    nc = ck_ref.shape[2]
    half = ck_ref.shape[3]
    for kv, (c_ref, o_ref) in enumerate(((ck_ref, ko_ref), (cv_ref, vo_ref))):
        c = c_ref[0, 0]
        pos = pos_ref[kv]
        a = _dot((c + pos[:, :half]).astype(BF16), w1_ref[kv, :half, :])
        b = _dot((c + pos[:, half:]).astype(BF16), w1_ref[kv, half:, :])
        hid = a + pltpu.roll(b, nc - 1, 0)
        y = _dot(jax.nn.gelu(hid, approximate=True).astype(BF16), w2_ref[kv])
        if kv == 0:
            y = _rope(y, cos_ref[...], sin_ref[...])
        o_ref[0, 0] = y.astype(o_ref.dtype)


def _nsa_compress(kc_in, vc_in, pos, w1, w2, cos_c, sin_c):
    B, G, T, d = kc_in.shape
    nc = T // CMP_STRIDE
    half = CMP_STRIDE * d
    ck = kc_in.reshape(B, G, nc, half)
    cv = vc_in.reshape(B, G, nc, half)
    hidden = w1.shape[-1]
    c_spec = pl.BlockSpec((1, 1, nc, half), lambda b, g: (b, g, 0, 0))
    o_spec = pl.BlockSpec((1, 1, nc, d), lambda b, g: (b, g, 0, 0))
    full = lambda shape: pl.BlockSpec(shape, lambda b, g: (0,) * len(shape))
    return pl.pallas_call(
        _nsa_compress_kernel,
        grid=(B, G),
        in_specs=[c_spec, c_spec, full((2, 1, 2 * half)), full((2, 2 * half, hidden)),
                  full((2, hidden, d)), full((nc, d)), full((nc, d))],
        out_specs=[o_spec, o_spec],
        out_shape=[jax.ShapeDtypeStruct((B, G, nc, d), BF16)] * 2,
        compiler_params=_cparams(("parallel", "parallel")),
        name="nsa_compress",
    )(ck, cv, pos.reshape(2, 1, 2 * half), w1.astype(BF16), w2.astype(BF16), cos_c, sin_c)


def _masked_keys(k, valid_t):
    return jnp.concatenate([k, jnp.where(valid_t, 0.0, NEG_INF).astype(BF16)], axis=1)


def _values_with_ones(v):
    return jnp.concatenate([v, jnp.ones(v.shape, BF16)], axis=1)


def _softmax_attend(q_aug, k_aug, v1):
    d = v1.shape[1] // 2
    s = _dot_nt(q_aug, k_aug)
    p = jnp.exp2((s - jnp.max(s, axis=1, keepdims=True)).astype(BF16))
    pv = _dot(p, v1)
    return pv[:, :d] / pv[:, d:]


def _nsa_attn_kernel(q_ref, kc_ref, vc_ref, ks_ref, vs_ref, kw_ref, vw_ref, gl_ref, ovt_ref,
                     o_ref, m_scr, acc_scr, ow_scr, sa_scr, sb_scr, *, hg, n_topk):
    tq = q_ref.shape[2]
    d = q_ref.shape[3]
    ncp = kc_ref.shape[2]
    ns = ovt_ref.shape[0]
    T = ks_ref.shape[2]
    R = hg * tq
    t0 = pl.program_id(2) * tq

    onehot = jnp.where(lax.broadcasted_iota(jnp.int32, (tq, tq), 0)
                       == lax.broadcasted_iota(jnp.int32, (tq, tq), 1), 1.0, 0.0).astype(BF16)
    q_aug = jnp.concatenate([q_ref[0].reshape(R, d), jnp.concatenate([onehot] * hg, axis=0)], axis=1)
    t_row = t0 + lax.broadcasted_iota(jnp.int32, (1, tq), 1)

    wk = min(WINDOW + tq, T)
    w0 = pl.multiple_of(jnp.maximum(t0 - WINDOW, 0), tq)
    kp = w0 + lax.broadcasted_iota(jnp.int32, (wk, tq), 0)
    kw_aug = _masked_keys(kw_ref[0, 0, pl.ds(w0, wk), :], (kp <= t_row) & (kp > t_row - WINDOW))
    vw = _values_with_ones(vw_ref[0, 0, pl.ds(w0, wk), :])
    ow_scr[...] = _softmax_attend(q_aug, kw_aug, vw)

    cmp_end = lax.broadcasted_iota(jnp.int32, (ncp, tq), 0) * CMP_STRIDE + (CMP_BLOCK - 1)
    kc_aug = _masked_keys(kc_ref[0, 0], cmp_end <= t_row)
    s = _dot_nt(q_aug, kc_aug)
    p = jnp.exp2(s - jnp.max(s, axis=1, keepdims=True))
    t_of_row = t0 + lax.broadcasted_iota(jnp.int32, (R, 1), 0) % tq
    inv_l = jnp.where(t_of_row >= CMP_BLOCK - 1, 1.0 / jnp.sum(p, axis=1, keepdims=True), 0.0)
    pc = p * inv_l
    o_c = _dot(pc.astype(BF16), vc_ref[0, 0])

    imp_t = _dot_nt_exact_by_f32(ovt_ref[...], jnp.sum(pc.reshape(hg, tq, ncp), axis=0))
    blk = lax.broadcasted_iota(jnp.int32, (ns, tq), 0)
    cur = t_row // SEL_BLOCK
    forced = (blk == 0) | (blk == cur) | (blk == cur - 1)
    x = jnp.where(forced, -jnp.inf, jnp.where(blk <= cur, imp_t, NEG_INF))
    blk_f = blk.astype(F32)
    sel = jnp.where(forced, 1.0, 0.0)
    for _ in range(n_topk - SEL_FORCED):
        mx = jnp.max(x, axis=0, keepdims=True)
        idx = jnp.min(jnp.where(x == mx, blk_f, float(ns)), axis=0, keepdims=True)
        hit = blk_f == idx
        sel = jnp.where(hit, 1.0, sel)
        x = jnp.where(hit, -jnp.inf, x)
    sel_b = sel.astype(BF16)

    cs = min(NSA_SEL_KEYS, T)
    m_scr[...] = jnp.full(m_scr.shape, NEG_INF, F32)
    acc_scr[...] = jnp.zeros(acc_scr.shape, F32)
    key_blk = lax.broadcasted_iota(jnp.int32, (cs, ns), 0) // SEL_BLOCK
    blk_col = lax.broadcasted_iota(jnp.int32, (cs, ns), 1)
    key_off = lax.broadcasted_iota(jnp.int32, (cs, tq), 0)

    n_chunks = (t0 + tq + cs - 1) // cs
    last_chunk = T // cs - 1

    def scores(c):
        k0 = pl.multiple_of(c * cs, cs)
        expand = jnp.where(key_blk + k0 // SEL_BLOCK == blk_col, 1.0, 0.0).astype(BF16)
        member = _dot(expand, sel_b)
        valid = (member > 0.5) & (key_off + k0 <= t_row)
        return _dot_nt(q_aug, _masked_keys(ks_ref[0, 0, pl.ds(k0, cs), :], valid))

    def attend(c, s_cur, s_next):
        if s_next is not None:
            s_next[...] = scores(jnp.minimum(c + 1, last_chunk))
        v1 = _values_with_ones(vs_ref[0, 0, pl.ds(pl.multiple_of(c * cs, cs), cs), :])
        s = s_cur[...]
        m_prev = m_scr[...]
        m_new = jnp.maximum(m_prev, jnp.max(s, axis=1, keepdims=True))
        alpha = jnp.exp2(m_prev - m_new)
        p = jnp.exp2((s - jnp.concatenate([m_new] * (cs // LANES), axis=1)).astype(BF16))
        acc_scr[...] = jnp.concatenate([alpha, alpha], axis=1) * acc_scr[...] + _dot(p, v1)
        m_scr[...] = m_new

    sa_scr[...] = scores(0)

    def sel_body(j, carry):
        attend(2 * j, sa_scr, sb_scr)
        attend(2 * j + 1, sb_scr, sa_scr)
        return carry

    lax.fori_loop(0, n_chunks // 2, sel_body, 0)

    @pl.when(n_chunks % 2 == 1)
    def _():
        attend(n_chunks - 1, sa_scr, None)

    o_s = acc_scr[:, :d] / acc_scr[:, d:]

    o_w = ow_scr[...]

    gate = _sigmoid(gl_ref[0])
    for h in range(hg):
        rows = slice(h * tq, (h + 1) * tq)
        out = (gate[:, h:h + 1] * o_c[rows]
               + gate[:, hg + h:hg + h + 1] * o_s[rows]
               + gate[:, 2 * hg + h:2 * hg + h + 1] * o_w[rows])
        o_ref[0, :, h * d:(h + 1) * d] = out.astype(o_ref.dtype)


def _nsa_attention(q, k_cmp, v_cmp, k_sel, v_sel, k_win, v_win, gate_logits, overlap_t):
    B, n_heads, T, d = q.shape
    G = NSA_KV_GROUPS
    hg = n_heads // G
    ncp = k_cmp.shape[2]
    ns = overlap_t.shape[0]
    tq = _pick(T, (NSA_Q_TILE,))
    R = hg * tq
    assert min(SEL_TOPK, ns) >= SEL_FORCED
    kv_spec = pl.BlockSpec((1, 1, T, d), lambda b, g, i: (b, g, 0, 0))
    cmp_spec = pl.BlockSpec((1, 1, ncp, d), lambda b, g, i: (b, g, 0, 0))
    return pl.pallas_call(
        functools.partial(_nsa_attn_kernel, hg=hg, n_topk=min(SEL_TOPK, ns)),
        grid=(B, G, T // tq),
        in_specs=[pl.BlockSpec((1, hg, tq, d), lambda b, g, i: (b, g, i, 0)),
                  cmp_spec, cmp_spec, kv_spec, kv_spec, kv_spec, kv_spec,
                  pl.BlockSpec((1, tq, LANES), lambda b, g, i: (b, i, g)),
                  pl.BlockSpec((ns, ncp), lambda b, g, i: (0, 0))],
        out_specs=pl.BlockSpec((1, tq, hg * d), lambda b, g, i: (b, i, g)),
        out_shape=jax.ShapeDtypeStruct((B, T, n_heads * d), BF16),
        scratch_shapes=[pltpu.VMEM((R, LANES), F32), pltpu.VMEM((R, 2 * d), F32),
                        pltpu.VMEM((R, d), F32),
                        pltpu.VMEM((R, min(NSA_SEL_KEYS, T)), F32),
                        pltpu.VMEM((R, min(NSA_SEL_KEYS, T)), F32)],
        compiler_params=_cparams(("parallel", "parallel", "parallel")),
        name="nsa_attention",
    )(q, k_cmp, v_cmp, k_sel, v_sel, k_win, v_win, gate_logits, overlap_t)


def _nsa_mixer(y, h_res, w_in, w_in_bf, cmp_pos, cmp_w1, cmp_w2, w_out, B, T):
    D = y.shape[1]
    d, G = HEAD_DIM, NSA_KV_GROUPS
    n_heads = D // d
    hg = n_heads // G
    kv_w = NSA_BRANCHES * 2 * G * d

    proj = _matmul(y, w_in_bf, F32, "nsa_in_proj", n=D + kv_w)
    wg = w_in[:, D + kv_w:].reshape(D, NSA_BRANCHES, G, hg).transpose(0, 2, 1, 3)
    wg = jnp.pad(wg.reshape(D, G, NSA_BRANCHES * hg), ((0, 0), (0, 0), (0, LANES - NSA_BRANCHES * hg)))
    gate_logits = _matmul(y, wg.reshape(D, G * LANES).astype(BF16), F32, "nsa_gate_proj")

    cos2, sin_s = _rope_tables(jnp.arange(T, dtype=F32))
    q, k_sel, v_sel, k_win, v_win, kc_in, vc_in = _nsa_prep(
        proj.reshape(B, T, D + kv_w), cos2, sin_s, n_heads, D)

    nc = T // CMP_STRIDE
    cmp_end = jnp.arange(nc, dtype=jnp.int32) * CMP_STRIDE + (CMP_BLOCK - 1)
    cos_c, sin_c = _rope_tables(cmp_end.astype(F32))
    k_cmp, v_cmp = _nsa_compress(kc_in, vc_in, cmp_pos, cmp_w1, cmp_w2, cos_c, sin_c)

    n_sel = T // SEL_BLOCK
    c_start = jnp.arange(nc)[None, :] * CMP_STRIDE
    s_start = jnp.arange(n_sel)[:, None] * SEL_BLOCK
    ov = jnp.minimum(c_start + CMP_BLOCK, s_start + SEL_BLOCK) - jnp.maximum(c_start, s_start)
    overlap_t = (jnp.clip(ov, 0, None).astype(F32) / CMP_BLOCK).astype(BF16)

    o = _nsa_attention(q, k_cmp, v_cmp, k_sel, v_sel, k_win, v_win,
                       gate_logits.reshape(B, T, G * LANES), overlap_t)
    return _matmul_residual(o.reshape(B * T, D), w_out.astype(BF16), h_res, "nsa_out_proj")


def _conv_silu_norm(x_ref, halo_ref, w_ref, xs_scr, o_scr, first, mode):
    tm = x_ref.shape[1]
    width = x_ref.shape[2]
    hr = CONV_HALO_ROWS
    xs_scr[0:hr, :] = jnp.where(first, 0.0, halo_ref[0])
    xs_scr[hr:hr + tm, :] = x_ref[0]
    acc = jnp.zeros((tm, width), F32)
    for j in range(CONV_K):
        r0 = hr - (CONV_K - 1) + j
        acc = acc + xs_scr[r0:r0 + tm, :] * w_ref[j:j + 1, :]
    y = _silu(acc)
    if mode == "v":
        o_scr[...] = y
        return
    scale = HEAD_DIM ** -0.5 if mode == "q" else 1.0
    for h in range(width // HEAD_DIM):
        yh = y[:, h * HEAD_DIM:(h + 1) * HEAD_DIM]
        ss = jnp.sum(yh * yh, axis=-1, keepdims=True)
        o_scr[:, h * HEAD_DIM:(h + 1) * HEAD_DIM] = yh * lax.rsqrt(ss + NORM_EPS) * scale


def _gdn_gate_kernel(ab_ref, alog_ref, dtb_ref, g_ref, beta_ref, *, n_heads):
    a = ab_ref[:, :n_heads]
    b = ab_ref[:, n_heads:2 * n_heads]
    x = a + dtb_ref[...]
    softplus = jnp.maximum(x, 0.0) + jnp.log(1.0 + jnp.exp(-jnp.abs(x)))
    g_ref[...] = -jnp.exp(alog_ref[...]) * softplus
    beta_ref[...] = _sigmoid(b)


def _gdn_gates(ab, a_log, dt_bias, n_heads):
    n = ab.shape[0]
    tm = _pick(n, (1024, 512, 256, 128))
    row = pl.BlockSpec((tm, n_heads), lambda i: (i, 0))
    vec = pl.BlockSpec((1, n_heads), lambda i: (0, 0))
    return pl.pallas_call(
        functools.partial(_gdn_gate_kernel, n_heads=n_heads),
        grid=(n // tm,),
        in_specs=[pl.BlockSpec((tm, ab.shape[1]), lambda i: (i, 0)), vec, vec],
        out_specs=[row, row],
        out_shape=[jax.ShapeDtypeStruct((n, n_heads), F32)] * 2,
        compiler_params=_cparams(("parallel",)),
        name="gdn_gates",
    )(ab, a_log.reshape(1, n_heads), dt_bias.reshape(1, n_heads))


def _gdn_delta_kernel(xq_ref, xk_ref, xv_ref, hq_ref, hk_ref, hv_ref, wq_ref, wk_ref, wv_ref,
                      z_ref, g_ref, beta_ref, e_ref, nw_ref, o_ref,
                      s_scr, xs_scr, q_scr, k_scr, v_scr, *, hb, nc):
    C, d = GDN_CHUNK, HEAD_DIM
    rows_all = nc * C

    first = pl.program_id(2) == 0

    @pl.when(first)
    def _():
        s_scr[...] = jnp.zeros(s_scr.shape, F32)

    _conv_silu_norm(xq_ref, hq_ref, wq_ref, xs_scr.at[0], q_scr, first, "q")
    _conv_silu_norm(xk_ref, hk_ref, wk_ref, xs_scr.at[1], k_scr, first, "k")
    _conv_silu_norm(xv_ref, hv_ref, wv_ref, xs_scr.at[2], v_scr, first, "v")

    row = lax.broadcasted_iota(jnp.int32, (C, C), 0)
    col = lax.broadcasted_iota(jnp.int32, (C, C), 1)
    tri = row >= col
    strict = row > col
    eye = row == col
    eye_f = jnp.where(eye, 1.0, 0.0)
    r2 = lax.broadcasted_iota(jnp.int32, (rows_all, rows_all), 0)
    c2 = lax.broadcasted_iota(jnp.int32, (rows_all, rows_all), 1)
    chunk_lower = jnp.where((r2 >= c2) & (r2 // C == c2 // C), 1.0, 0.0).astype(BF16)

    gc = _dot_exact_by_f32(chunk_lower, g_ref[0])
    gc_b = _dot_f32_by_exact(gc, e_ref[0])
    beta_b = _dot_f32_by_exact(beta_ref[0], e_ref[0])

    pairs = [(c, h) for c in range(nc) for h in range(hb)]
    loc = []
    for c, h in pairs:
        rs, cols = slice(c * C, (c + 1) * C), slice(h * d, (h + 1) * d)
        q, k, v = q_scr[rs, cols], k_scr[rs, cols], v_scr[rs, cols]
        gcb, bb = gc_b[rs, cols], beta_b[rs, cols]
        gc_row = jnp.sum(jnp.where(eye, gcb[:, :C], 0.0), axis=0, keepdims=True)
        decay = jnp.exp(jnp.where(tri, gcb[:, :C] - gc_row, -jnp.inf))
        kb = k.astype(BF16)
        a_mat = jnp.where(strict, bb[:, :C] * _dot_nt(kb, kb) * decay, 0.0)
        loc.append(dict(q=q, k=k, v=v, gcb=gcb, bb=bb, decay=decay, kb=kb, a=a_mat))

    t_inv = [eye_f - p["a"] for p in loc]
    pw = [_dot_bf(p["a"], p["a"]) for p in loc]
    n_sq = max(int(math.ceil(math.log2(C))) - 1, 0)
    for i in range(n_sq):
        t_inv = [t + _dot_bf(t, p) for t, p in zip(t_inv, pw)]
        if i + 1 < n_sq:
            pw = [_dot_bf(p, p) for p in pw]

    uw = []
    for p, t in zip(loc, t_inv):
        p["eg"] = jnp.exp(p["gcb"])
        rhs = jnp.concatenate([p["v"] * p["bb"], p["k"] * (p["bb"] * p["eg"])], axis=1)
        uw.append(_dot(t.astype(BF16), rhs.astype(BF16)))
    wu_b = [jnp.concatenate([x[:, d:], x[:, :d]], axis=1).astype(BF16) for x in uw]
    qk_wu = [_dot((_dot_nt(p["q"].astype(BF16), p["kb"]) * p["decay"]).astype(BF16), wu)
             for p, wu in zip(loc, wu_b)]
    kd_wu = []
    for p, wu in zip(loc, wu_b):
        p["gc_last"] = p["gcb"][C - 1:C, :]
        k_dec = p["k"] * jnp.exp(p["gc_last"] - p["gcb"])
        kd_wu.append(_dot(k_dec.T.astype(BF16), wu))

    states = [s_scr[h] for h in range(hb)]
    for c in range(nc):
        idx = [c * hb + h for h in range(hb)]
        lhs = [jnp.concatenate([kd_wu[i][:, :d], loc[i]["q"] * loc[i]["eg"] - qk_wu[i][:, :d]],
                               axis=0).astype(BF16) for i in idx]
        prod = [_dot(l, s.astype(BF16)) for l, s in zip(lhs, states)]
        for h, i in enumerate(idx):
            o = prod[h][d:] + qk_wu[i][:, d:]
            states[h] = states[h] * jnp.exp(loc[i]["gc_last"]) - prod[h][:d] + kd_wu[i][:, d:]
            ms = jnp.mean(o * o, axis=-1, keepdims=True)
            y = o * lax.rsqrt(ms + NORM_EPS) * nw_ref[...]
            rs, cols = slice(c * C, (c + 1) * C), slice(h * d, (h + 1) * d)
            o_ref[0, rs, cols] = (y * _silu(z_ref[0, rs, cols])).astype(o_ref.dtype)
    for h in range(hb):
        s_scr[h] = states[h]


def _gdn_delta(proj, conv_w, g, beta, norm_w, n_heads):
    B, T, _ = proj.shape
    d, C = HEAD_DIM, GDN_CHUNK
    kw = n_heads * d
    hb = _pick(n_heads, (GDN_HEADS_PER_STEP, 2, 1))
    nc = _pick(T // C, (GDN_CHUNKS_PER_STEP, 2, 1))
    rows, width, hr = nc * C, hb * d, CONV_HALO_ROWS
    n_hg = n_heads // hb
    expand = (jnp.arange(n_heads)[None, :, None]
              == (jnp.arange(n_hg)[:, None, None] * hb + jnp.arange(width)[None, None, :] // d))
    expand = expand.astype(BF16)

    def cols(part):
        return pl.BlockSpec((1, rows, width), lambda b, hg, c: (b, c, part * n_hg + hg))

    def halo(part):
        return pl.BlockSpec((1, hr, width),
                            lambda b, hg, c: (b, jnp.maximum(c * (rows // hr) - 1, 0), part * n_hg + hg))

    def taps(part):
        return pl.BlockSpec((CONV_K, width), lambda b, hg, c: (0, part * n_hg + hg))

    gate_spec = pl.BlockSpec((1, rows, n_heads), lambda b, hg, c: (b, c, 0))
    return pl.pallas_call(
        functools.partial(_gdn_delta_kernel, hb=hb, nc=nc),
        grid=(B, n_hg, T // rows),
        in_specs=[cols(0), cols(1), cols(2), halo(0), halo(1), halo(2), taps(0), taps(1), taps(2),
                  cols(3), gate_spec, gate_spec,
                  pl.BlockSpec((1, n_heads, width), lambda b, hg, c: (hg, 0, 0)),
                  pl.BlockSpec((1, d), lambda b, hg, c: (0, 0))],
        out_specs=pl.BlockSpec((1, rows, width), lambda b, hg, c: (b, c, hg)),
        out_shape=jax.ShapeDtypeStruct((B, T, kw), BF16),
        scratch_shapes=[pltpu.VMEM((hb, d, d), F32), pltpu.VMEM((3, hr + rows, width), F32),
                        pltpu.VMEM((rows, width), F32), pltpu.VMEM((rows, width), F32),
                        pltpu.VMEM((rows, width), F32)],
        compiler_params=_cparams(("parallel", "parallel", "arbitrary")),
        name="gdn_delta",
    )(proj, proj, proj, proj, proj, proj, conv_w, conv_w, conv_w, proj,
      g.reshape(B, T, n_heads), beta.reshape(B, T, n_heads), expand, norm_w.reshape(1, d))


def _gdn_mixer(y, h_res, w_in, w_in_bf, conv_w, a_log, dt_bias, norm_w, w_out, B, T):
    D = y.shape[1]
    n_heads = D // HEAD_DIM
    kw = n_heads * HEAD_DIM
    proj = _matmul(y, w_in_bf, F32, "gdn_in_proj", n=4 * kw).reshape(B, T, 4 * kw)
    w_ab = jnp.pad(w_in[:, 4 * kw:], ((0, 0), (0, LANES - 2 * n_heads)))
    ab = _matmul(y, w_ab.astype(BF16), F32, "gdn_ab_proj")
    g, beta = _gdn_gates(ab, a_log, dt_bias, n_heads)
    o = _gdn_delta(proj, conv_w, g, beta, norm_w, n_heads)
    return _matmul_residual(o.reshape(B * T, kw), w_out.astype(BF16), h_res, "gdn_out_proj")


def kernel(x, norm_mix, norm_ffn, norm_final, nsa_w_in, nsa_cmp_pos, nsa_cmp_w1, nsa_cmp_w2,
           nsa_w_out, gdn_w_in, gdn_conv, gdn_a_log, gdn_dt_bias, gdn_norm, gdn_w_out,
           ffn_w_gate, ffn_w_up, ffn_w_down):
    B, T, D = x.shape
    depth = norm_mix.shape[0]
    h = x.reshape(B * T, D)
    nsa_w_in_bf = nsa_w_in.astype(BF16)
    gdn_w_in_bf = gdn_w_in.astype(BF16)
    for layer in range(depth):
        y = _rmsnorm(h, norm_mix[layer], BF16)
        i = layer // 2
        if layer % 2 == 0:
            h = _nsa_mixer(y, h, nsa_w_in[i], nsa_w_in_bf[i], nsa_cmp_pos[i], nsa_cmp_w1[i],
                           nsa_cmp_w2[i], nsa_w_out[i], B, T)
        else:
            h = _gdn_mixer(y, h, gdn_w_in[i], gdn_w_in_bf[i], gdn_conv[i], gdn_a_log[i],
                           gdn_dt_bias[i], gdn_norm[i], gdn_w_out[i], B, T)
        y = _rmsnorm(h, norm_ffn[layer], BF16)
        hid = _gateup(y, ffn_w_gate[layer].astype(BF16), ffn_w_up[layer].astype(BF16))
        h = _matmul_residual(hid, ffn_w_down[layer].astype(BF16), h, "ffn_down")
    return _rmsnorm(h, norm_final, F32).reshape(B, T, D)
```

```python
import functools
import math

import jax
import jax.numpy as jnp
from jax import lax
from jax.experimental import pallas as pl
from jax.experimental.pallas import tpu as pltpu

F32 = jnp.float32
BF16 = jnp.bfloat16

HEAD_DIM = 128
ROPE_THETA = 10000.0
NORM_EPS = 1e-6
NSA_KV_GROUPS = 4
NSA_BRANCHES = 3
CMP_STRIDE = 16
CMP_BLOCK = 2 * CMP_STRIDE
SEL_BLOCK = 64
SEL_TOPK = 16
SEL_FORCED = 3
WINDOW = 512
NEG_INF = -1e30
FORCE_SCORE = 1e30
CONV_K = 4
GDN_CHUNK = 64
LOG2_E = 1.4426950408889634

V7X_VMEM_LIMIT_BYTES = 56 * 1024 * 1024
LANES = 128
CONV_HALO_ROWS = 8
NSA_Q_TILE = 128
NSA_SEL_KEYS = 512
GDN_HEADS_PER_STEP = 8
GDN_CHUNKS_PER_STEP = 2


def _cparams(sem):
    return pltpu.CompilerParams(dimension_semantics=sem, vmem_limit_bytes=V7X_VMEM_LIMIT_BYTES)


def _pick(n, prefs):
    for p in prefs:
        if p <= n and n % p == 0:
            return p
    return n


def _dot(a, b):
    return jnp.dot(a, b, preferred_element_type=F32)


def _dot_nt(a, b):
    return lax.dot_general(a, b, (((1,), (1,)), ((), ())), preferred_element_type=F32)


def _split3(x):
    x1 = x.astype(BF16)
    r1 = x - x1.astype(F32)
    x2 = r1.astype(BF16)
    r2 = r1 - x2.astype(F32)
    return x1, x2, r2.astype(BF16)


def _dot_f32_by_exact(x, e):
    x1, x2, x3 = _split3(x)
    return _dot(x1, e) + _dot(x2, e) + _dot(x3, e)


def _dot_exact_by_f32(e, x):
    x1, x2, x3 = _split3(x)
    return _dot(e, x1) + _dot(e, x2) + _dot(e, x3)


def _dot_nt_exact_by_f32(e, x):
    x1, x2, x3 = _split3(x)
    return _dot_nt(e, x1) + _dot_nt(e, x2) + _dot_nt(e, x3)


def _dot_bf(a, b):
    return _dot(a.astype(BF16), b.astype(BF16))


def _sigmoid(x):
    return 1.0 / (1.0 + jnp.exp(-x))


def _silu(x):
    return x * _sigmoid(x)


def _rmsnorm_kernel(x_ref, w_ref, o_ref):
    x = x_ref[...]
    ms = jnp.mean(x * x, axis=-1, keepdims=True)
    o_ref[...] = (x * lax.rsqrt(ms + NORM_EPS) * w_ref[...]).astype(o_ref.dtype)


def _rmsnorm(x, w, out_dtype):
    n, d = x.shape
    tm = _pick(n, (256, 128, 64, 32, 16, 8))
    return pl.pallas_call(
        _rmsnorm_kernel,
        grid=(n // tm,),
        in_specs=[pl.BlockSpec((tm, d), lambda i: (i, 0)),
                  pl.BlockSpec((1, d), lambda i: (0, 0))],
        out_specs=pl.BlockSpec((tm, d), lambda i: (i, 0)),
        out_shape=jax.ShapeDtypeStruct((n, d), out_dtype),
        compiler_params=_cparams(("parallel",)),
        name="rmsnorm",
    )(x, w.reshape(1, d))


def _mm_kernel(x_ref, w_ref, o_ref):
    o_ref[...] = _dot(x_ref[...], w_ref[...]).astype(o_ref.dtype)


def _weight_spec(w, layer, block, index):
    if w.ndim == 2:
        return pl.BlockSpec(block, index)
    return pl.BlockSpec((None,) + block, lambda *g: (layer,) + index(*g))


def _matmul(x, w, out_dtype, name, n=None, layer=None):
    m, k = x.shape
    n = w.shape[-1] if n is None else n
    tm = _pick(m, (1024, 512, 256, 128))
    tn = _pick(n, (512, 256, 128))
    return pl.pallas_call(
        _mm_kernel,
        grid=(m // tm, n // tn),
        in_specs=[pl.BlockSpec((tm, k), lambda i, j: (i, 0)),
                  _weight_spec(w, layer, (k, tn), lambda i, j: (0, j))],
        out_specs=pl.BlockSpec((tm, tn), lambda i, j: (i, j)),
        out_shape=jax.ShapeDtypeStruct((m, n), out_dtype),
        compiler_params=_cparams(("parallel", "parallel")),
        name=name,
    )(x, w)


def _mm_res_kernel(x_ref, w_ref, r_ref, o_ref):
    @pl.when(pl.program_id(2) == 0)
    def _():
        o_ref[...] = r_ref[...]

    o_ref[...] += _dot(x_ref[...], w_ref[...])


def _matmul_residual(x, w, res, name, layer=None):
    m, k = x.shape
    n = w.shape[-1]
    tm = _pick(m, (1024, 512, 256, 128))
    tn = _pick(n, (512, 256, 128))
    tk = k if k <= 4096 else _pick(k, (5504, 4096, 2816, 2048, 1408, 1024, 512, 256, 128))
    return pl.pallas_call(
        _mm_res_kernel,
        grid=(m // tm, n // tn, k // tk),
        in_specs=[pl.BlockSpec((tm, tk), lambda i, j, l: (i, l)),
                  _weight_spec(w, layer, (tk, tn), lambda i, j, l: (l, j)),
                  pl.BlockSpec((tm, tn), lambda i, j, l: (i, j))],
        out_specs=pl.BlockSpec((tm, tn), lambda i, j, l: (i, j)),
        out_shape=jax.ShapeDtypeStruct((m, n), F32),
        compiler_params=_cparams(("parallel", "parallel", "arbitrary")),
        name=name,
    )(x, w, res)


def _gateup_kernel(x_ref, wg_ref, wu_ref, o_ref):
    x = x_ref[...]
    g = _dot(x, wg_ref[...])
    u = _dot(x, wu_ref[...])
    o_ref[...] = (_silu(g) * u).astype(o_ref.dtype)


def _gateup(x, wg, wu, layer=None):
    m, k = x.shape
    n = wg.shape[-1]
    tm = _pick(m, (1024, 512, 256, 128))
    tn = _pick(n, (512, 256, 128))
    return pl.pallas_call(
        _gateup_kernel,
        grid=(m // tm, n // tn),
        in_specs=[pl.BlockSpec((tm, k), lambda i, j: (i, 0)),
                  _weight_spec(wg, layer, (k, tn), lambda i, j: (0, j)),
                  _weight_spec(wu, layer, (k, tn), lambda i, j: (0, j))],
        out_specs=pl.BlockSpec((tm, tn), lambda i, j: (i, j)),
        out_shape=jax.ShapeDtypeStruct((m, n), BF16),
        compiler_params=_cparams(("parallel", "parallel")),
        name="ffn_gateup",
    )(x, wg, wu)


def _rope_tables(pos):
    inv = ROPE_THETA ** (-jnp.arange(0, HEAD_DIM, 2, dtype=F32) / HEAD_DIM)
    ang = pos[:, None] * inv[None, :]
    c, s = jnp.cos(ang), jnp.sin(ang)
    return jnp.concatenate([c, c], axis=-1), jnp.concatenate([-s, s], axis=-1)


def _rope(x, cos2, sin_signed):
    half = HEAD_DIM // 2
    rot = jnp.concatenate([x[:, half:], x[:, :half]], axis=-1)
    return x * cos2 + rot * sin_signed


def _nsa_prep_kernel(p_ref, cos_ref, sin_ref, q_ref, ks_ref, vs_ref, kw_ref, vw_ref,
                     kc_ref, vc_ref, *, n_heads, d_model):
    G, d = NSA_KV_GROUPS, HEAD_DIM
    cos2 = cos_ref[...]
    sin_s = sin_ref[...]
    scale = d ** -0.5 * LOG2_E
    for h in range(n_heads):
        x = p_ref[0, :, h * d:(h + 1) * d]
        q_ref[0, h] = (_rope(x, cos2, sin_s) * scale).astype(q_ref.dtype)

    def kv_cols(branch, kv, g):
        c0 = d_model + ((branch * 2 + kv) * G + g) * d
        return p_ref[0, :, c0:c0 + d]

    for g in range(G):
        kc_ref[0, g] = kv_cols(0, 0, g)
        vc_ref[0, g] = kv_cols(0, 1, g)
        ks_ref[0, g] = _rope(kv_cols(1, 0, g), cos2, sin_s).astype(ks_ref.dtype)
        vs_ref[0, g] = kv_cols(1, 1, g).astype(vs_ref.dtype)
        kw_ref[0, g] = _rope(kv_cols(2, 0, g), cos2, sin_s).astype(kw_ref.dtype)
        vw_ref[0, g] = kv_cols(2, 1, g).astype(vw_ref.dtype)


def _nsa_prep(proj, cos2, sin_s, n_heads, d_model):
    B, T, W = proj.shape
    G, d = NSA_KV_GROUPS, HEAD_DIM
    tm = _pick(T, (256, 128, 64))
    kv_spec = pl.BlockSpec((1, G, tm, d), lambda b, i: (b, 0, i, 0))
    kv_bf = jax.ShapeDtypeStruct((B, G, T, d), BF16)
    kv_f32 = jax.ShapeDtypeStruct((B, G, T, d), F32)
    return pl.pallas_call(
        functools.partial(_nsa_prep_kernel, n_heads=n_heads, d_model=d_model),
        grid=(B, T // tm),
        in_specs=[pl.BlockSpec((1, tm, W), lambda b, i: (b, i, 0)),
                  pl.BlockSpec((tm, d), lambda b, i: (i, 0)),
                  pl.BlockSpec((tm, d), lambda b, i: (i, 0))],
        out_specs=[pl.BlockSpec((1, n_heads, tm, d), lambda b, i: (b, 0, i, 0)),
                   kv_spec, kv_spec, kv_spec, kv_spec, kv_spec, kv_spec],
        out_shape=[jax.ShapeDtypeStruct((B, n_heads, T, d), BF16),
                   kv_bf, kv_bf, kv_bf, kv_bf, kv_f32, kv_f32],
        compiler_params=_cparams(("parallel", "parallel")),
        name="nsa_prep",
    )(proj, cos2, sin_s)


def _nsa_compress_kernel(ck_ref, cv_ref, pos_ref, w1_ref, w2_ref, cos_ref, sin_ref,
                         ko_ref, vo_ref):
    nc = ck_ref.shape[2]
    half = ck_ref.shape[3]
    for kv, (c_ref, o_ref) in enumerate(((ck_ref, ko_ref), (cv_ref, vo_ref))):
        c = c_ref[0, 0]
        pos = pos_ref[kv]
        a = _dot((c + pos[:, :half]).astype(BF16), w1_ref[kv, :half, :])
        b = _dot((c + pos[:, half:]).astype(BF16), w1_ref[kv, half:, :])
        hid = a + pltpu.roll(b, nc - 1, 0)
        y = _dot(jax.nn.gelu(hid, approximate=True).astype(BF16), w2_ref[kv])
        if kv == 0:
            y = _rope(y, cos_ref[...], sin_ref[...])
        o_ref[0, 0] = y.astype(o_ref.dtype)


def _nsa_compress(kc_in, vc_in, pos, w1, w2, cos_c, sin_c):
    B, G, T, d = kc_in.shape
    nc = T // CMP_STRIDE
    half = CMP_STRIDE * d
    ck = kc_in.reshape(B, G, nc, half)
    cv = vc_in.reshape(B, G, nc, half)
    hidden = w1.shape[-1]
    c_spec = pl.BlockSpec((1, 1, nc, half), lambda b, g: (b, g, 0, 0))
    o_spec = pl.BlockSpec((1, 1, nc, d), lambda b, g: (b, g, 0, 0))
    full = lambda shape: pl.BlockSpec(shape, lambda b, g: (0,) * len(shape))
    return pl.pallas_call(
        _nsa_compress_kernel,
        grid=(B, G),
        in_specs=[c_spec, c_spec, full((2, 1, 2 * half)), full((2, 2 * half, hidden)),
                  full((2, hidden, d)), full((nc, d)), full((nc, d))],
        out_specs=[o_spec, o_spec],
        out_shape=[jax.ShapeDtypeStruct((B, G, nc, d), BF16)] * 2,
        compiler_params=_cparams(("parallel", "parallel")),
        name="nsa_compress",
    )(ck, cv, pos.reshape(2, 1, 2 * half), w1.astype(BF16), w2.astype(BF16), cos_c, sin_c)


def _masked_keys(k, valid_t):
    return jnp.concatenate([k, jnp.where(valid_t, 0.0, NEG_INF).astype(BF16)], axis=1)


def _values_with_ones(v):
    return jnp.concatenate([v, jnp.ones(v.shape, BF16)], axis=1)


def _softmax_attend(q_aug, k_aug, v1):
    d = v1.shape[1] // 2
    s = _dot_nt(q_aug, k_aug)
    p = jnp.exp2((s - jnp.max(s, axis=1, keepdims=True)).astype(BF16))
    pv = _dot(p, v1)
    return pv[:, :d] / pv[:, d:]


def _nsa_attn_kernel(q_ref, kc_ref, vc_ref, ks_ref, vs_ref, kw_ref, vw_ref, gl_ref, ovt_ref,
                     o_ref, m_scr, acc_scr, ow_scr, sa_scr, sb_scr, *, hg, n_topk):
    tq = q_ref.shape[2]
    d = q_ref.shape[3]
    ncp = kc_ref.shape[2]
    ns = ovt_ref.shape[0]
    T = ks_ref.shape[2]
    R = hg * tq
    t0 = pl.program_id(2) * tq

    onehot = jnp.where(lax.broadcasted_iota(jnp.int32, (tq, tq), 0)
                       == lax.broadcasted_iota(jnp.int32, (tq, tq), 1), 1.0, 0.0).astype(BF16)
    q_aug = jnp.concatenate([q_ref[0].reshape(R, d), jnp.concatenate([onehot] * hg, axis=0)], axis=1)
    t_row = t0 + lax.broadcasted_iota(jnp.int32, (1, tq), 1)

    wk = min(WINDOW + tq, T)
    w0 = pl.multiple_of(jnp.maximum(t0 - WINDOW, 0), tq)
    kp = w0 + lax.broadcasted_iota(jnp.int32, (wk, tq), 0)
    kw_aug = _masked_keys(kw_ref[0, 0, pl.ds(w0, wk), :], (kp <= t_row) & (kp > t_row - WINDOW))
    vw = _values_with_ones(vw_ref[0, 0, pl.ds(w0, wk), :])
    ow_scr[...] = _softmax_attend(q_aug, kw_aug, vw)

    cmp_end = lax.broadcasted_iota(jnp.int32, (ncp, tq), 0) * CMP_STRIDE + (CMP_BLOCK - 1)
    kc_aug = _masked_keys(kc_ref[0, 0], cmp_end <= t_row)
    s = _dot_nt(q_aug, kc_aug)
    p = jnp.exp2(s - jnp.max(s, axis=1, keepdims=True))
    t_of_row = t0 + lax.broadcasted_iota(jnp.int32, (R, 1), 0) % tq
    inv_l = jnp.where(t_of_row >= CMP_BLOCK - 1, 1.0 / jnp.sum(p, axis=1, keepdims=True), 0.0)
    pc = p * inv_l
    o_c = _dot(pc.astype(BF16), vc_ref[0, 0])

    imp_t = _dot_nt_exact_by_f32(ovt_ref[...], jnp.sum(pc.reshape(hg, tq, ncp), axis=0))
    blk = lax.broadcasted_iota(jnp.int32, (ns, tq), 0)
    cur = t_row // SEL_BLOCK
    forced = (blk == 0) | (blk == cur) | (blk == cur - 1)
    x = jnp.where(forced, -jnp.inf, jnp.where(blk <= cur, imp_t, NEG_INF))
    blk_f = blk.astype(F32)
    sel = jnp.where(forced, 1.0, 0.0)
    for _ in range(n_topk - SEL_FORCED):
        mx = jnp.max(x, axis=0, keepdims=True)
        idx = jnp.min(jnp.where(x == mx, blk_f, float(ns)), axis=0, keepdims=True)
        hit = blk_f == idx
        sel = jnp.where(hit, 1.0, sel)
        x = jnp.where(hit, -jnp.inf, x)
    sel_b = sel.astype(BF16)

    cs = min(NSA_SEL_KEYS, T)
    m_scr[...] = jnp.full(m_scr.shape, NEG_INF, F32)
    acc_scr[...] = jnp.zeros(acc_scr.shape, F32)
    key_blk = lax.broadcasted_iota(jnp.int32, (cs, ns), 0) // SEL_BLOCK
    blk_col = lax.broadcasted_iota(jnp.int32, (cs, ns), 1)
    key_off = lax.broadcasted_iota(jnp.int32, (cs, tq), 0)

    n_chunks = (t0 + tq + cs - 1) // cs
    last_chunk = T // cs - 1

    def scores(c):
        k0 = pl.multiple_of(c * cs, cs)
        expand = jnp.where(key_blk + k0 // SEL_BLOCK == blk_col, 1.0, 0.0).astype(BF16)
        member = _dot(expand, sel_b)
        valid = (member > 0.5) & (key_off + k0 <= t_row)
        return _dot_nt(q_aug, _masked_keys(ks_ref[0, 0, pl.ds(k0, cs), :], valid))

    def attend(c, s_cur, s_next):
        if s_next is not None:
            s_next[...] = scores(jnp.minimum(c + 1, last_chunk))
        v1 = _values_with_ones(vs_ref[0, 0, pl.ds(pl.multiple_of(c * cs, cs), cs), :])
        s = s_cur[...]
        m_prev = m_scr[...]
        m_new = jnp.maximum(m_prev, jnp.max(s, axis=1, keepdims=True))
        alpha = jnp.exp2(m_prev - m_new)
        p = jnp.exp2((s - jnp.concatenate([m_new] * (cs // LANES), axis=1)).astype(BF16))
        acc_scr[...] = jnp.concatenate([alpha, alpha], axis=1) * acc_scr[...] + _dot(p, v1)
        m_scr[...] = m_new

    sa_scr[...] = scores(0)

    def pair(c):
        attend(c, sa_scr, sb_scr)
        attend(c + 1, sb_scr, sa_scr)

    def sel_body(j, carry):
        pair(4 * j)
        pair(4 * j + 2)
        return carry

    lax.fori_loop(0, n_chunks // 4, sel_body, 0)
    done = (n_chunks // 4) * 4

    @pl.when(n_chunks - done >= 2)
    def _():
        pair(done)

    @pl.when(n_chunks % 2 == 1)
    def _():
        attend(n_chunks - 1, sa_scr, None)

    o_s = acc_scr[:, :d] / acc_scr[:, d:]

    o_w = ow_scr[...]

    gate = _sigmoid(gl_ref[0])
    for h in range(hg):
        rows = slice(h * tq, (h + 1) * tq)
        out = (gate[:, h:h + 1] * o_c[rows]
               + gate[:, hg + h:hg + h + 1] * o_s[rows]
               + gate[:, 2 * hg + h:2 * hg + h + 1] * o_w[rows])
        o_ref[0, :, h * d:(h + 1) * d] = out.astype(o_ref.dtype)


def _nsa_attention(q, k_cmp, v_cmp, k_sel, v_sel, k_win, v_win, gate_logits, overlap_t):
    B, n_heads, T, d = q.shape
    G = NSA_KV_GROUPS
    hg = n_heads // G
    ncp = k_cmp.shape[2]
    ns = overlap_t.shape[0]
    tq = _pick(T, (NSA_Q_TILE,))
    R = hg * tq
    assert min(SEL_TOPK, ns) >= SEL_FORCED
    kv_spec = pl.BlockSpec((1, 1, T, d), lambda b, g, i: (b, g, 0, 0))
    cmp_spec = pl.BlockSpec((1, 1, ncp, d), lambda b, g, i: (b, g, 0, 0))
    return pl.pallas_call(
        functools.partial(_nsa_attn_kernel, hg=hg, n_topk=min(SEL_TOPK, ns)),
        grid=(B, G, T // tq),
        in_specs=[pl.BlockSpec((1, hg, tq, d), lambda b, g, i: (b, g, i, 0)),
                  cmp_spec, cmp_spec, kv_spec, kv_spec, kv_spec, kv_spec,
                  pl.BlockSpec((1, tq, LANES), lambda b, g, i: (b, i, g)),
                  pl.BlockSpec((ns, ncp), lambda b, g, i: (0, 0))],
        out_specs=pl.BlockSpec((1, tq, hg * d), lambda b, g, i: (b, i, g)),
        out_shape=jax.ShapeDtypeStruct((B, T, n_heads * d), BF16),
        scratch_shapes=[pltpu.VMEM((R, LANES), F32), pltpu.VMEM((R, 2 * d), F32),
                        pltpu.VMEM((R, d), F32),
                        pltpu.VMEM((R, min(NSA_SEL_KEYS, T)), F32),
                        pltpu.VMEM((R, min(NSA_SEL_KEYS, T)), F32)],
        compiler_params=_cparams(("parallel", "parallel", "parallel")),
        name="nsa_attention",
    )(q, k_cmp, v_cmp, k_sel, v_sel, k_win, v_win, gate_logits, overlap_t)


def _nsa_mixer(y, h_res, w_in, w_in_bf, cmp_pos, cmp_w1, cmp_w2, w_out_bf, layer, B, T):
    D = y.shape[1]
    d, G = HEAD_DIM, NSA_KV_GROUPS
    n_heads = D // d
    hg = n_heads // G
    kv_w = NSA_BRANCHES * 2 * G * d

    proj = _matmul(y, w_in_bf, F32, "nsa_in_proj", n=D + kv_w, layer=layer)
    wg = w_in[:, D + kv_w:].reshape(D, NSA_BRANCHES, G, hg).transpose(0, 2, 1, 3)
    wg = jnp.pad(wg.reshape(D, G, NSA_BRANCHES * hg), ((0, 0), (0, 0), (0, LANES - NSA_BRANCHES * hg)))
    gate_logits = _matmul(y, wg.reshape(D, G * LANES).astype(BF16), F32, "nsa_gate_proj")

    cos2, sin_s = _rope_tables(jnp.arange(T, dtype=F32))
    q, k_sel, v_sel, k_win, v_win, kc_in, vc_in = _nsa_prep(
        proj.reshape(B, T, D + kv_w), cos2, sin_s, n_heads, D)

    nc = T // CMP_STRIDE
    cmp_end = jnp.arange(nc, dtype=jnp.int32) * CMP_STRIDE + (CMP_BLOCK - 1)
    cos_c, sin_c = _rope_tables(cmp_end.astype(F32))
    k_cmp, v_cmp = _nsa_compress(kc_in, vc_in, cmp_pos, cmp_w1, cmp_w2, cos_c, sin_c)

    n_sel = T // SEL_BLOCK
    c_start = jnp.arange(nc)[None, :] * CMP_STRIDE
    s_start = jnp.arange(n_sel)[:, None] * SEL_BLOCK
    ov = jnp.minimum(c_start + CMP_BLOCK, s_start + SEL_BLOCK) - jnp.maximum(c_start, s_start)
    overlap_t = (jnp.clip(ov, 0, None).astype(F32) / CMP_BLOCK).astype(BF16)

    o = _nsa_attention(q, k_cmp, v_cmp, k_sel, v_sel, k_win, v_win,
                       gate_logits.reshape(B, T, G * LANES), overlap_t)
    return _matmul_residual(o.reshape(B * T, D), w_out_bf, h_res, "nsa_out_proj", layer)


def _conv_silu_norm(x_ref, halo_ref, w_ref, xs_scr, o_scr, first, mode):
    tm = x_ref.shape[1]
    width = x_ref.shape[2]
    hr = CONV_HALO_ROWS
    xs_scr[0:hr, :] = jnp.where(first, 0.0, halo_ref[0])
    xs_scr[hr:hr + tm, :] = x_ref[0]
    acc = jnp.zeros((tm, width), F32)
    for j in range(CONV_K):
        r0 = hr - (CONV_K - 1) + j
        acc = acc + xs_scr[r0:r0 + tm, :] * w_ref[j:j + 1, :]
    y = _silu(acc)
    if mode == "v":
        o_scr[...] = y
        return
    scale = HEAD_DIM ** -0.5 if mode == "q" else 1.0
    for h in range(width // HEAD_DIM):
        yh = y[:, h * HEAD_DIM:(h + 1) * HEAD_DIM]
        ss = jnp.sum(yh * yh, axis=-1, keepdims=True)
        o_scr[:, h * HEAD_DIM:(h + 1) * HEAD_DIM] = yh * lax.rsqrt(ss + NORM_EPS) * scale


def _gdn_gate_kernel(ab_ref, alog_ref, dtb_ref, g_ref, beta_ref, *, n_heads):
    a = ab_ref[:, :n_heads]
    b = ab_ref[:, n_heads:2 * n_heads]
    x = a + dtb_ref[...]
    softplus = jnp.maximum(x, 0.0) + jnp.log(1.0 + jnp.exp(-jnp.abs(x)))
    g_ref[...] = -jnp.exp(alog_ref[...]) * softplus
    beta_ref[...] = _sigmoid(b)


def _gdn_gates(ab, a_log, dt_bias, n_heads):
    n = ab.shape[0]
    tm = _pick(n, (1024, 512, 256, 128))
    row = pl.BlockSpec((tm, n_heads), lambda i: (i, 0))
    vec = pl.BlockSpec((1, n_heads), lambda i: (0, 0))
    return pl.pallas_call(
        functools.partial(_gdn_gate_kernel, n_heads=n_heads),
        grid=(n // tm,),
        in_specs=[pl.BlockSpec((tm, ab.shape[1]), lambda i: (i, 0)), vec, vec],
        out_specs=[row, row],
        out_shape=[jax.ShapeDtypeStruct((n, n_heads), F32)] * 2,
        compiler_params=_cparams(("parallel",)),
        name="gdn_gates",
    )(ab, a_log.reshape(1, n_heads), dt_bias.reshape(1, n_heads))


def _gdn_delta_kernel(xq_ref, xk_ref, xv_ref, hq_ref, hk_ref, hv_ref, wq_ref, wk_ref, wv_ref,
                      z_ref, g_ref, beta_ref, e_ref, nw_ref, o_ref,
                      s_scr, xs_scr, q_scr, k_scr, v_scr, *, hb, nc):
    C, d = GDN_CHUNK, HEAD_DIM
    rows_all = nc * C

    first = pl.program_id(2) == 0

    @pl.when(first)
    def _():
        s_scr[...] = jnp.zeros(s_scr.shape, F32)

    _conv_silu_norm(xq_ref, hq_ref, wq_ref, xs_scr.at[0], q_scr, first, "q")
    _conv_silu_norm(xk_ref, hk_ref, wk_ref, xs_scr.at[1], k_scr, first, "k")
    _conv_silu_norm(xv_ref, hv_ref, wv_ref, xs_scr.at[2], v_scr, first, "v")

    row = lax.broadcasted_iota(jnp.int32, (C, C), 0)
    col = lax.broadcasted_iota(jnp.int32, (C, C), 1)
    tri = row >= col
    strict = row > col
    eye = row == col
    eye_f = jnp.where(eye, 1.0, 0.0)
    r2 = lax.broadcasted_iota(jnp.int32, (rows_all, rows_all), 0)
    c2 = lax.broadcasted_iota(jnp.int32, (rows_all, rows_all), 1)
    chunk_lower = jnp.where((r2 >= c2) & (r2 // C == c2 // C), 1.0, 0.0).astype(BF16)

    gc = _dot_exact_by_f32(chunk_lower, g_ref[0])
    gc_b = _dot_f32_by_exact(gc, e_ref[0])
    beta_b = _dot_f32_by_exact(beta_ref[0], e_ref[0])

    pairs = [(c, h) for c in range(nc) for h in range(hb)]
    loc = []
    for c, h in pairs:
        rs, cols = slice(c * C, (c + 1) * C), slice(h * d, (h + 1) * d)
        q, k, v = q_scr[rs, cols], k_scr[rs, cols], v_scr[rs, cols]
        gcb, bb = gc_b[rs, cols], beta_b[rs, cols]
        gc_row = jnp.sum(jnp.where(eye, gcb[:, :C], 0.0), axis=0, keepdims=True)
        decay = jnp.exp(jnp.where(tri, gcb[:, :C] - gc_row, -jnp.inf))
        kb = k.astype(BF16)
        a_mat = jnp.where(strict, bb[:, :C] * _dot_nt(kb, kb) * decay, 0.0)
        loc.append(dict(q=q, k=k, v=v, gcb=gcb, bb=bb, decay=decay, kb=kb, a=a_mat))

    t_inv = [eye_f - p["a"] for p in loc]
    pw = [_dot_bf(p["a"], p["a"]) for p in loc]
    n_sq = max(int(math.ceil(math.log2(C))) - 1, 0)
    for i in range(n_sq):
        t_inv = [t + _dot_bf(t, p) for t, p in zip(t_inv, pw)]
        if i + 1 < n_sq:
            pw = [_dot_bf(p, p) for p in pw]

    uw = []
    for p, t in zip(loc, t_inv):
        p["eg"] = jnp.exp(p["gcb"])
        rhs = jnp.concatenate([p["v"] * p["bb"], p["k"] * (p["bb"] * p["eg"])], axis=1)
        uw.append(_dot(t.astype(BF16), rhs.astype(BF16)))
    wu_b = [jnp.concatenate([x[:, d:], x[:, :d]], axis=1).astype(BF16) for x in uw]
    qk_wu = [_dot((_dot_nt(p["q"].astype(BF16), p["kb"]) * p["decay"]).astype(BF16), wu)
             for p, wu in zip(loc, wu_b)]
    kd_wu = []
    for p, wu in zip(loc, wu_b):
        p["gc_last"] = p["gcb"][C - 1:C, :]
        k_dec = p["k"] * jnp.exp(p["gc_last"] - p["gcb"])
        kd_wu.append(_dot(k_dec.T.astype(BF16), wu))

    states = [s_scr[h] for h in range(hb)]
    for c in range(nc):
        idx = [c * hb + h for h in range(hb)]
        lhs = [jnp.concatenate([kd_wu[i][:, :d], loc[i]["q"] * loc[i]["eg"] - qk_wu[i][:, :d]],
                               axis=0).astype(BF16) for i in idx]
        prod = [_dot(l, s.astype(BF16)) for l, s in zip(lhs, states)]
        for h, i in enumerate(idx):
            o = prod[h][d:] + qk_wu[i][:, d:]
            states[h] = states[h] * jnp.exp(loc[i]["gc_last"]) - prod[h][:d] + kd_wu[i][:, d:]
            ms = jnp.mean(o * o, axis=-1, keepdims=True)
            y = o * lax.rsqrt(ms + NORM_EPS) * nw_ref[...]
            rs, cols = slice(c * C, (c + 1) * C), slice(h * d, (h + 1) * d)
            o_ref[0, rs, cols] = (y * _silu(z_ref[0, rs, cols])).astype(o_ref.dtype)
    for h in range(hb):
        s_scr[h] = states[h]


def _gdn_delta(proj, conv_w, g, beta, norm_w, n_heads):
    B, T, _ = proj.shape
    d, C = HEAD_DIM, GDN_CHUNK
    kw = n_heads * d
    hb = _pick(n_heads, (GDN_HEADS_PER_STEP, 2, 1))
    nc = _pick(T // C, (GDN_CHUNKS_PER_STEP, 2, 1))
    rows, width, hr = nc * C, hb * d, CONV_HALO_ROWS
    n_hg = n_heads // hb
    expand = (jnp.arange(n_heads)[None, :, None]
              == (jnp.arange(n_hg)[:, None, None] * hb + jnp.arange(width)[None, None, :] // d))
    expand = expand.astype(BF16)

    def cols(part):
        return pl.BlockSpec((1, rows, width), lambda b, hg, c: (b, c, part * n_hg + hg))

    def halo(part):
        return pl.BlockSpec((1, hr, width),
                            lambda b, hg, c: (b, jnp.maximum(c * (rows // hr) - 1, 0), part * n_hg + hg))

    def taps(part):
        return pl.BlockSpec((CONV_K, width), lambda b, hg, c: (0, part * n_hg + hg))

    gate_spec = pl.BlockSpec((1, rows, n_heads), lambda b, hg, c: (b, c, 0))
    return pl.pallas_call(
        functools.partial(_gdn_delta_kernel, hb=hb, nc=nc),
        grid=(B, n_hg, T // rows),
        in_specs=[cols(0), cols(1), cols(2), halo(0), halo(1), halo(2), taps(0), taps(1), taps(2),
                  cols(3), gate_spec, gate_spec,
                  pl.BlockSpec((1, n_heads, width), lambda b, hg, c: (hg, 0, 0)),
                  pl.BlockSpec((1, d), lambda b, hg, c: (0, 0))],
        out_specs=pl.BlockSpec((1, rows, width), lambda b, hg, c: (b, c, hg)),
        out_shape=jax.ShapeDtypeStruct((B, T, kw), BF16),
        scratch_shapes=[pltpu.VMEM((hb, d, d), F32), pltpu.VMEM((3, hr + rows, width), F32),
                        pltpu.VMEM((rows, width), F32), pltpu.VMEM((rows, width), F32),
                        pltpu.VMEM((rows, width), F32)],
        compiler_params=_cparams(("parallel", "parallel", "arbitrary")),
        name="gdn_delta",
    )(proj, proj, proj, proj, proj, proj, conv_w, conv_w, conv_w, proj,
      g.reshape(B, T, n_heads), beta.reshape(B, T, n_heads), expand, norm_w.reshape(1, d))


def _gdn_mixer(y, h_res, w_in, w_in_bf, conv_w, a_log, dt_bias, norm_w, w_out_bf, layer, B, T):
    D = y.shape[1]
    n_heads = D // HEAD_DIM
    kw = n_heads * HEAD_DIM
    proj = _matmul(y, w_in_bf, F32, "gdn_in_proj", n=4 * kw, layer=layer).reshape(B, T, 4 * kw)
    w_ab = jnp.pad(w_in[:, 4 * kw:], ((0, 0), (0, LANES - 2 * n_heads)))
    ab = _matmul(y, w_ab.astype(BF16), F32, "gdn_ab_proj")
    g, beta = _gdn_gates(ab, a_log, dt_bias, n_heads)
    o = _gdn_delta(proj, conv_w, g, beta, norm_w, n_heads)
    return _matmul_residual(o.reshape(B * T, kw), w_out_bf, h_res, "gdn_out_proj", layer)


def kernel(x, norm_mix, norm_ffn, norm_final, nsa_w_in, nsa_cmp_pos, nsa_cmp_w1, nsa_cmp_w2,
           nsa_w_out, gdn_w_in, gdn_conv, gdn_a_log, gdn_dt_bias, gdn_norm, gdn_w_out,
           ffn_w_gate, ffn_w_up, ffn_w_down):
    B, T, D = x.shape
    depth = norm_mix.shape[0]
    h = x.reshape(B * T, D)
    bf = lambda w: w.astype(BF16)
    nsa_w_in_bf, nsa_w_out_bf = bf(nsa_w_in), bf(nsa_w_out)
    gdn_w_in_bf, gdn_w_out_bf = bf(gdn_w_in), bf(gdn_w_out)
    w_gate_bf, w_up_bf, w_down_bf = bf(ffn_w_gate), bf(ffn_w_up), bf(ffn_w_down)
    for layer in range(depth):
        y = _rmsnorm(h, norm_mix[layer], BF16)
        i = layer // 2
        if layer % 2 == 0:
            h = _nsa_mixer(y, h, nsa_w_in[i], nsa_w_in_bf, nsa_cmp_pos[i], nsa_cmp_w1[i],
                           nsa_cmp_w2[i], nsa_w_out_bf, i, B, T)
        else:
            h = _gdn_mixer(y, h, gdn_w_in[i], gdn_w_in_bf, gdn_conv[i], gdn_a_log[i],
                           gdn_dt_bias[i], gdn_norm[i], gdn_w_out_bf, i, B, T)
        y = _rmsnorm(h, norm_ffn[layer], BF16)
        hid = _gateup(y, w_gate_bf, w_up_bf, layer)
        h = _matmul_residual(hid, w_down_bf, h, "ffn_down", layer)
    return _rmsnorm(h, norm_final, F32).reshape(B, T, D)
```

```python
import functools
import math

import jax
import jax.numpy as jnp
from jax import lax
from jax.experimental import pallas as pl
from jax.experimental.pallas import tpu as pltpu

F32 = jnp.float32
BF16 = jnp.bfloat16

HEAD_DIM = 128
ROPE_THETA = 10000.0
NORM_EPS = 1e-6
NSA_KV_GROUPS = 4
NSA_BRANCHES = 3
CMP_STRIDE = 16
CMP_BLOCK = 2 * CMP_STRIDE
SEL_BLOCK = 64
SEL_TOPK = 16
SEL_FORCED = 3
WINDOW = 512
NEG_INF = -1e30
FORCE_SCORE = 1e30
CONV_K = 4
GDN_CHUNK = 64
LOG2_E = 1.4426950408889634

V7X_VMEM_LIMIT_BYTES = 56 * 1024 * 1024
LANES = 128
CONV_HALO_ROWS = 8
NSA_Q_TILE = 128
NSA_SEL_KEYS = 512
NORM_ROWS_PER_PASS = 128
GDN_HEADS_PER_STEP = 8
GDN_CHUNKS_PER_STEP = 4


def _cparams(sem):
    return pltpu.CompilerParams(dimension_semantics=sem, vmem_limit_bytes=V7X_VMEM_LIMIT_BYTES)


def _pick(n, prefs):
    for p in prefs:
        if p <= n and n % p == 0:
            return p
    return n


def _dot(a, b):
    return jnp.dot(a, b, preferred_element_type=F32)


def _dot_nt(a, b):
    return lax.dot_general(a, b, (((1,), (1,)), ((), ())), preferred_element_type=F32)


def _split3(x):
    x1 = x.astype(BF16)
    r1 = x - x1.astype(F32)
    x2 = r1.astype(BF16)
    r2 = r1 - x2.astype(F32)
    return x1, x2, r2.astype(BF16)


def _dot_f32_by_exact(x, e):
    x1, x2, x3 = _split3(x)
    return _dot(x1, e) + _dot(x2, e) + _dot(x3, e)


def _dot_exact_by_f32(e, x):
    x1, x2, x3 = _split3(x)
    return _dot(e, x1) + _dot(e, x2) + _dot(e, x3)


def _dot_nt_exact_by_f32(e, x):
    x1, x2, x3 = _split3(x)
    return _dot_nt(e, x1) + _dot_nt(e, x2) + _dot_nt(e, x3)


def _dot_bf(a, b):
    return _dot(a.astype(BF16), b.astype(BF16))


def _sigmoid(x):
    return 1.0 / (1.0 + jnp.exp(-x))


def _silu(x):
    return x * _sigmoid(x)


def _rmsnorm_kernel(x_ref, w_ref, o_ref):
    x = x_ref[...]
    ms = jnp.mean(x * x, axis=-1, keepdims=True)
    o_ref[...] = (x * lax.rsqrt(ms + NORM_EPS) * w_ref[...]).astype(o_ref.dtype)


def _rmsnorm(x, w, out_dtype):
    n, d = x.shape
    tm = _pick(n, (256, 128, 64, 32, 16, 8))
    return pl.pallas_call(
        _rmsnorm_kernel,
        grid=(n // tm,),
        in_specs=[pl.BlockSpec((tm, d), lambda i: (i, 0)),
                  pl.BlockSpec((1, d), lambda i: (0, 0))],
        out_specs=pl.BlockSpec((tm, d), lambda i: (i, 0)),
        out_shape=jax.ShapeDtypeStruct((n, d), out_dtype),
        compiler_params=_cparams(("parallel",)),
        name="rmsnorm",
    )(x, w.reshape(1, d))


def _stream_init_kernel(x_ref, xb_ref, ss_ref):
    x = x_ref[...]
    xb_ref[...] = x.astype(xb_ref.dtype)
    ss_ref[0] = jnp.sum(x * x, axis=1, keepdims=True)


def _stream_init(x):
    n, d = x.shape
    tm = _pick(n, (256, 128, 64, 32, 16, 8))
    return pl.pallas_call(
        _stream_init_kernel,
        grid=(n // tm,),
        in_specs=[pl.BlockSpec((tm, d), lambda i: (i, 0))],
        out_specs=[pl.BlockSpec((tm, d), lambda i: (i, 0)),
                   pl.BlockSpec((1, tm, 1), lambda i: (0, i, 0))],
        out_shape=[jax.ShapeDtypeStruct((n, d), BF16), jax.ShapeDtypeStruct((1, n, 1), F32)],
        compiler_params=_cparams(("parallel",)),
        name="stream_init",
    )(x)


def _inv_rms(ss_ref, d):
    return lax.rsqrt(jnp.sum(ss_ref[...], axis=0) / d + NORM_EPS)


def _normalize_rows(x_ref, ss_ref, xn_scr):
    tm, k = x_ref.shape
    rows = _pick(tm, (NORM_ROWS_PER_PASS,))

    @pl.when(pl.program_id(1) == 0)
    def _():
        def body(r, carry):
            rs = pl.ds(pl.multiple_of(r * rows, rows), rows)
            scale = lax.rsqrt(jnp.sum(ss_ref[:, rs, :], axis=0) / k + NORM_EPS)
            xn_scr[rs, :] = (x_ref[rs, :].astype(F32) * scale).astype(xn_scr.dtype)
            return carry

        lax.fori_loop(0, tm // rows, body, 0)


def _mm_kernel(x_ref, w_ref, ss_ref, o_ref, xn_scr):
    _normalize_rows(x_ref, ss_ref, xn_scr)
    o_ref[...] = _dot(xn_scr[...], w_ref[...]).astype(o_ref.dtype)


def _mm_narrow_kernel(x_ref, w_ref, ss_ref, o_ref):
    o_ref[...] = (_dot(x_ref[...], w_ref[...]) * _inv_rms(ss_ref, x_ref.shape[1])).astype(o_ref.dtype)


def _weight_spec(w, layer, block, index):
    if w.ndim == 2:
        return pl.BlockSpec(block, index)
    return pl.BlockSpec((None,) + block, lambda *g: (layer,) + index(*g))


def _ss_spec(ss, tm):
    return pl.BlockSpec((ss.shape[0], tm, 1), lambda i, *_: (0, i, 0))


def _matmul(stream, w, out_dtype, name, n=None, layer=None):
    x, ss = stream
    m, k = x.shape
    n = w.shape[-1] if n is None else n
    tm = _pick(m, (1024, 512, 256, 128))
    tn = _pick(n, (512, 256, 128))
    narrow = n < k
    return pl.pallas_call(
        _mm_narrow_kernel if narrow else _mm_kernel,
        grid=(m // tm, n // tn),
        in_specs=[pl.BlockSpec((tm, k), lambda i, j: (i, 0)),
                  _weight_spec(w, layer, (k, tn), lambda i, j: (0, j)),
                  _ss_spec(ss, tm)],
        out_specs=pl.BlockSpec((tm, tn), lambda i, j: (i, j)),
        out_shape=jax.ShapeDtypeStruct((m, n), out_dtype),
        scratch_shapes=[] if narrow else [pltpu.VMEM((tm, k), BF16)],
        compiler_params=_cparams(("parallel", "arbitrary")),
        name=name,
    )(x, w, ss)


def _mm_res_kernel(x_ref, w_ref, r_ref, o_ref, ob_ref, ss_ref):
    @pl.when(pl.program_id(2) == 0)
    def _():
        o_ref[...] = r_ref[...]

    o_ref[...] += _dot(x_ref[...], w_ref[...])

    @pl.when(pl.program_id(2) == pl.num_programs(2) - 1)
    def _():
        o = o_ref[...]
        ob_ref[...] = o.astype(ob_ref.dtype)
        ss_ref[0] = jnp.sum(o * o, axis=1, keepdims=True)


def _matmul_residual(x, w, res, name, layer=None):
    m, k = x.shape
    n = w.shape[-1]
    tm = _pick(m, (1024, 512, 256, 128))
    tn = _pick(n, (512, 256, 128))
    tk = k if k <= 4096 else _pick(k, (5504, 4096, 2816, 2048, 1408, 1024, 512, 256, 128))
    tile = pl.BlockSpec((tm, tn), lambda i, j, l: (i, j))
    h, hb, ss = pl.pallas_call(
        _mm_res_kernel,
        grid=(m // tm, n // tn, k // tk),
        in_specs=[pl.BlockSpec((tm, tk), lambda i, j, l: (i, l)),
                  _weight_spec(w, layer, (tk, tn), lambda i, j, l: (l, j)),
                  tile],
        out_specs=[tile, tile, pl.BlockSpec((1, tm, 1), lambda i, j, l: (j, i, 0))],
        out_shape=[jax.ShapeDtypeStruct((m, n), F32), jax.ShapeDtypeStruct((m, n), BF16),
                   jax.ShapeDtypeStruct((n // tn, m, 1), F32)],
        compiler_params=_cparams(("parallel", "parallel", "arbitrary")),
        name=name,
    )(x, w, res)
    return h, (hb, ss)


def _gateup_kernel(x_ref, wg_ref, wu_ref, ss_ref, o_ref, xn_scr):
    _normalize_rows(x_ref, ss_ref, xn_scr)
    x = xn_scr[...]
    g = _dot(x, wg_ref[...])
    u = _dot(x, wu_ref[...])
    o_ref[...] = (_silu(g) * u).astype(o_ref.dtype)


def _gateup(stream, wg, wu, layer=None):
    x, ss = stream
    m, k = x.shape
    n = wg.shape[-1]
    tm = _pick(m, (1024, 512, 256, 128))
    tn = _pick(n, (512, 256, 128))
    return pl.pallas_call(
        _gateup_kernel,
        grid=(m // tm, n // tn),
        in_specs=[pl.BlockSpec((tm, k), lambda i, j: (i, 0)),
                  _weight_spec(wg, layer, (k, tn), lambda i, j: (0, j)),
                  _weight_spec(wu, layer, (k, tn), lambda i, j: (0, j)),
                  _ss_spec(ss, tm)],
        out_specs=pl.BlockSpec((tm, tn), lambda i, j: (i, j)),
        out_shape=jax.ShapeDtypeStruct((m, n), BF16),
        scratch_shapes=[pltpu.VMEM((tm, k), BF16)],
        compiler_params=_cparams(("parallel", "arbitrary")),
        name="ffn_gateup",
    )(x, wg, wu, ss)


def _rope_tables(pos):
    inv = ROPE_THETA ** (-jnp.arange(0, HEAD_DIM, 2, dtype=F32) / HEAD_DIM)
    ang = pos[:, None] * inv[None, :]
    c, s = jnp.cos(ang), jnp.sin(ang)
    return jnp.concatenate([c, c], axis=-1), jnp.concatenate([-s, s], axis=-1)


def _rope(x, cos2, sin_signed):
    half = HEAD_DIM // 2
    rot = jnp.concatenate([x[:, half:], x[:, :half]], axis=-1)
    return x * cos2 + rot * sin_signed


def _nsa_prep_kernel(p_ref, cos_ref, sin_ref, q_ref, ks_ref, vs_ref, kw_ref, vw_ref,
                     kc_ref, vc_ref, *, n_heads, d_model):
    G, d = NSA_KV_GROUPS, HEAD_DIM
    cos2 = cos_ref[...]
    sin_s = sin_ref[...]
    scale = d ** -0.5 * LOG2_E
    for h in range(n_heads):
        x = p_ref[0, :, h * d:(h + 1) * d]
        q_ref[0, h] = (_rope(x, cos2, sin_s) * scale).astype(q_ref.dtype)

    def kv_cols(branch, kv, g):
        c0 = d_model + ((branch * 2 + kv) * G + g) * d
        return p_ref[0, :, c0:c0 + d]

    for g in range(G):
        kc_ref[0, g] = kv_cols(0, 0, g)
        vc_ref[0, g] = kv_cols(0, 1, g)
        ks_ref[0, g] = _rope(kv_cols(1, 0, g), cos2, sin_s).astype(ks_ref.dtype)
        vs_ref[0, g] = kv_cols(1, 1, g).astype(vs_ref.dtype)
        kw_ref[0, g] = _rope(kv_cols(2, 0, g), cos2, sin_s).astype(kw_ref.dtype)
        vw_ref[0, g] = kv_cols(2, 1, g).astype(vw_ref.dtype)


def _nsa_prep(proj, cos2, sin_s, n_heads, d_model):
    B, T, W = proj.shape
    G, d = NSA_KV_GROUPS, HEAD_DIM
    tm = _pick(T, (256, 128, 64))
    kv_spec = pl.BlockSpec((1, G, tm, d), lambda b, i: (b, 0, i, 0))
    kv_bf = jax.ShapeDtypeStruct((B, G, T, d), BF16)
    kv_f32 = jax.ShapeDtypeStruct((B, G, T, d), F32)
    return pl.pallas_call(
        functools.partial(_nsa_prep_kernel, n_heads=n_heads, d_model=d_model),
        grid=(B, T // tm),
        in_specs=[pl.BlockSpec((1, tm, W), lambda b, i: (b, i, 0)),
                  pl.BlockSpec((tm, d), lambda b, i: (i, 0)),
                  pl.BlockSpec((tm, d), lambda b, i: (i, 0))],
        out_specs=[pl.BlockSpec((1, n_heads, tm, d), lambda b, i: (b, 0, i, 0)),
                   kv_spec, kv_spec, kv_spec, kv_spec, kv_spec, kv_spec],
        out_shape=[jax.ShapeDtypeStruct((B, n_heads, T, d), BF16),
                   kv_bf, kv_bf, kv_bf, kv_bf, kv_f32, kv_f32],
        compiler_params=_cparams(("parallel", "parallel")),
        name="nsa_prep",
    )(proj, cos2, sin_s)


def _nsa_compress_kernel(ck_ref, cv_ref, pos_ref, w1_ref, w2_ref, cos_ref, sin_ref,
                         ko_ref, vo_ref):
    nc = ck_ref.shape[2]
    half = ck_ref.shape[3]
    for kv, (c_ref, o_ref) in enumerate(((ck_ref, ko_ref), (cv_ref, vo_ref))):
        c = c_ref[0, 0]
        pos = pos_ref[kv]
        a = _dot((c + pos[:, :half]).astype(BF16), w1_ref[kv, :half, :])
        b = _dot((c + pos[:, half:]).astype(BF16), w1_ref[kv, half:, :])
        hid = a + pltpu.roll(b, nc - 1, 0)
        y = _dot(jax.nn.gelu(hid, approximate=True).astype(BF16), w2_ref[kv])
        if kv == 0:
            y = _rope(y, cos_ref[...], sin_ref[...])
        o_ref[0, 0] = y.astype(o_ref.dtype)


def _nsa_compress(kc_in, vc_in, pos, w1, w2, cos_c, sin_c):
    B, G, T, d = kc_in.shape
    nc = T // CMP_STRIDE
    half = CMP_STRIDE * d
    ck = kc_in.reshape(B, G, nc, half)
    cv = vc_in.reshape(B, G, nc, half)
    hidden = w1.shape[-1]
    c_spec = pl.BlockSpec((1, 1, nc, half), lambda b, g: (b, g, 0, 0))
    o_spec = pl.BlockSpec((1, 1, nc, d), lambda b, g: (b, g, 0, 0))
    full = lambda shape: pl.BlockSpec(shape, lambda b, g: (0,) * len(shape))
    return pl.pallas_call(
        _nsa_compress_kernel,
        grid=(B, G),
        in_specs=[c_spec, c_spec, full((2, 1, 2 * half)), full((2, 2 * half, hidden)),
                  full((2, hidden, d)), full((nc, d)), full((nc, d))],
        out_specs=[o_spec, o_spec],
        out_shape=[jax.ShapeDtypeStruct((B, G, nc, d), BF16)] * 2,
        compiler_params=_cparams(("parallel", "parallel")),
        name="nsa_compress",
    )(ck, cv, pos.reshape(2, 1, 2 * half), w1.astype(BF16), w2.astype(BF16), cos_c, sin_c)


def _masked_keys(k, valid_t):
    return jnp.concatenate([k, jnp.where(valid_t, 0.0, NEG_INF).astype(BF16)], axis=1)


def _values_with_ones(v):
    return jnp.concatenate([v, jnp.ones(v.shape, BF16)], axis=1)


def _softmax_attend(q_aug, k_aug, v1):
    d = v1.shape[1] // 2
    s = _dot_nt(q_aug, k_aug)
    p = jnp.exp2((s - jnp.max(s, axis=1, keepdims=True)).astype(BF16))
    pv = _dot(p, v1)
    return pv[:, :d] / pv[:, d:]


def _nsa_attn_kernel(q_ref, kc_ref, vc_ref, ks_ref, vs_ref, kw_ref, vw_ref, gl_ref, ovt_ref,
                     o_ref, m_scr, acc_scr, ow_scr, sa_scr, sb_scr, *, hg, n_topk):
    tq = q_ref.shape[2]
    d = q_ref.shape[3]
    ncp = kc_ref.shape[2]
    ns = ovt_ref.shape[0]
    T = ks_ref.shape[2]
    R = hg * tq
    t0 = pl.program_id(2) * tq

    onehot = jnp.where(lax.broadcasted_iota(jnp.int32, (tq, tq), 0)
                       == lax.broadcasted_iota(jnp.int32, (tq, tq), 1), 1.0, 0.0).astype(BF16)
    q_aug = jnp.concatenate([q_ref[0].reshape(R, d), jnp.concatenate([onehot] * hg, axis=0)], axis=1)
    t_row = t0 + lax.broadcasted_iota(jnp.int32, (1, tq), 1)

    wk = min(WINDOW + tq, T)
    w0 = pl.multiple_of(jnp.maximum(t0 - WINDOW, 0), tq)
    kp = w0 + lax.broadcasted_iota(jnp.int32, (wk, tq), 0)
    kw_aug = _masked_keys(kw_ref[0, 0, pl.ds(w0, wk), :], (kp <= t_row) & (kp > t_row - WINDOW))
    vw = _values_with_ones(vw_ref[0, 0, pl.ds(w0, wk), :])
    ow_scr[...] = _softmax_attend(q_aug, kw_aug, vw)

    cmp_end = lax.broadcasted_iota(jnp.int32, (ncp, tq), 0) * CMP_STRIDE + (CMP_BLOCK - 1)
    kc_aug = _masked_keys(kc_ref[0, 0], cmp_end <= t_row)
    s = _dot_nt(q_aug, kc_aug)
    p = jnp.exp2(s - jnp.max(s, axis=1, keepdims=True))
    t_of_row = t0 + lax.broadcasted_iota(jnp.int32, (R, 1), 0) % tq
    inv_l = jnp.where(t_of_row >= CMP_BLOCK - 1, 1.0 / jnp.sum(p, axis=1, keepdims=True), 0.0)
    pc = p * inv_l
    o_c = _dot(pc.astype(BF16), vc_ref[0, 0])

    imp_t = _dot_nt_exact_by_f32(ovt_ref[...], jnp.sum(pc.reshape(hg, tq, ncp), axis=0))
    blk = lax.broadcasted_iota(jnp.int32, (ns, tq), 0)
    cur = t_row // SEL_BLOCK
    forced = (blk == 0) | (blk == cur) | (blk == cur - 1)
    x = jnp.where(forced, -jnp.inf, jnp.where(blk <= cur, imp_t, NEG_INF))
    blk_f = blk.astype(F32)
    sel = jnp.where(forced, 1.0, 0.0)
    for _ in range(n_topk - SEL_FORCED):
        mx = jnp.max(x, axis=0, keepdims=True)
        idx = jnp.min(jnp.where(x == mx, blk_f, float(ns)), axis=0, keepdims=True)
        hit = blk_f == idx
        sel = jnp.where(hit, 1.0, sel)
        x = jnp.where(hit, -jnp.inf, x)
    sel_b = sel.astype(BF16)

    cs = min(NSA_SEL_KEYS, T)
    m_scr[...] = jnp.full(m_scr.shape, NEG_INF, F32)
    acc_scr[...] = jnp.zeros(acc_scr.shape, F32)
    key_blk = lax.broadcasted_iota(jnp.int32, (cs, ns), 0) // SEL_BLOCK
    blk_col = lax.broadcasted_iota(jnp.int32, (cs, ns), 1)
    key_off = lax.broadcasted_iota(jnp.int32, (cs, tq), 0)

    n_chunks = (t0 + tq + cs - 1) // cs
    last_chunk = T // cs - 1

    def scores(c):
        k0 = pl.multiple_of(c * cs, cs)
        expand = jnp.where(key_blk + k0 // SEL_BLOCK == blk_col, 1.0, 0.0).astype(BF16)
        member = _dot(expand, sel_b)
        valid = (member > 0.5) & (key_off + k0 <= t_row)
        return _dot_nt(q_aug, _masked_keys(ks_ref[0, 0, pl.ds(k0, cs), :], valid))

    def attend(c, s_cur, s_next):
        if s_next is not None:
            s_next[...] = scores(jnp.minimum(c + 1, last_chunk))
        v1 = _values_with_ones(vs_ref[0, 0, pl.ds(pl.multiple_of(c * cs, cs), cs), :])
        s = s_cur[...]
        m_prev = m_scr[...]
        m_new = jnp.maximum(m_prev, jnp.max(s, axis=1, keepdims=True))
        alpha = jnp.exp2(m_prev - m_new)
        p = jnp.exp2((s - jnp.concatenate([m_new] * (cs // LANES), axis=1)).astype(BF16))
        acc_scr[...] = jnp.concatenate([alpha, alpha], axis=1) * acc_scr[...] + _dot(p, v1)
        m_scr[...] = m_new

    sa_scr[...] = scores(0)

    def pair(c):
        attend(c, sa_scr, sb_scr)
        attend(c + 1, sb_scr, sa_scr)

    def sel_body(j, carry):
        pair(4 * j)
        pair(4 * j + 2)
        return carry

    lax.fori_loop(0, n_chunks // 4, sel_body, 0)
    done = (n_chunks // 4) * 4

    @pl.when(n_chunks - done >= 2)
    def _():
        pair(done)

    @pl.when(n_chunks % 2 == 1)
    def _():
        attend(n_chunks - 1, sa_scr, None)

    o_s = acc_scr[:, :d] / acc_scr[:, d:]

    o_w = ow_scr[...]

    gate = _sigmoid(gl_ref[0])
    for h in range(hg):
        rows = slice(h * tq, (h + 1) * tq)
        out = (gate[:, h:h + 1] * o_c[rows]
               + gate[:, hg + h:hg + h + 1] * o_s[rows]
               + gate[:, 2 * hg + h:2 * hg + h + 1] * o_w[rows])
        o_ref[0, :, h * d:(h + 1) * d] = out.astype(o_ref.dtype)


def _nsa_attention(q, k_cmp, v_cmp, k_sel, v_sel, k_win, v_win, gate_logits, overlap_t):
    B, n_heads, T, d = q.shape
    G = NSA_KV_GROUPS
    hg = n_heads // G
    ncp = k_cmp.shape[2]
    ns = overlap_t.shape[0]
    tq = _pick(T, (NSA_Q_TILE,))
    R = hg * tq
    assert min(SEL_TOPK, ns) >= SEL_FORCED
    kv_spec = pl.BlockSpec((1, 1, T, d), lambda b, g, i: (b, g, 0, 0))
    cmp_spec = pl.BlockSpec((1, 1, ncp, d), lambda b, g, i: (b, g, 0, 0))
    return pl.pallas_call(
        functools.partial(_nsa_attn_kernel, hg=hg, n_topk=min(SEL_TOPK, ns)),
        grid=(B, G, T // tq),
        in_specs=[pl.BlockSpec((1, hg, tq, d), lambda b, g, i: (b, g, i, 0)),
                  cmp_spec, cmp_spec, kv_spec, kv_spec, kv_spec, kv_spec,
                  pl.BlockSpec((1, tq, LANES), lambda b, g, i: (b, i, g)),
                  pl.BlockSpec((ns, ncp), lambda b, g, i: (0, 0))],
        out_specs=pl.BlockSpec((1, tq, hg * d), lambda b, g, i: (b, i, g)),
        out_shape=jax.ShapeDtypeStruct((B, T, n_heads * d), BF16),
        scratch_shapes=[pltpu.VMEM((R, LANES), F32), pltpu.VMEM((R, 2 * d), F32),
                        pltpu.VMEM((R, d), F32),
                        pltpu.VMEM((R, min(NSA_SEL_KEYS, T)), F32),
                        pltpu.VMEM((R, min(NSA_SEL_KEYS, T)), F32)],
        compiler_params=_cparams(("parallel", "parallel", "parallel")),
        name="nsa_attention",
    )(q, k_cmp, v_cmp, k_sel, v_sel, k_win, v_win, gate_logits, overlap_t)


def _nsa_mixer(y, h_res, gain, w_in, w_in_bf, cmp_pos, cmp_w1, cmp_w2, w_out_bf, layer, B, T):
    D = h_res.shape[1]
    d, G = HEAD_DIM, NSA_KV_GROUPS
    n_heads = D // d
    hg = n_heads // G
    kv_w = NSA_BRANCHES * 2 * G * d

    proj = _matmul(y, w_in_bf, F32, "nsa_in_proj", n=D + kv_w, layer=layer)
    wg = (gain[:, None] * w_in[:, D + kv_w:]).reshape(D, NSA_BRANCHES, G, hg).transpose(0, 2, 1, 3)
    wg = jnp.pad(wg.reshape(D, G, NSA_BRANCHES * hg), ((0, 0), (0, 0), (0, LANES - NSA_BRANCHES * hg)))
    gate_logits = _matmul(y, wg.reshape(D, G * LANES).astype(BF16), F32, "nsa_gate_proj")

    cos2, sin_s = _rope_tables(jnp.arange(T, dtype=F32))
    q, k_sel, v_sel, k_win, v_win, kc_in, vc_in = _nsa_prep(
        proj.reshape(B, T, D + kv_w), cos2, sin_s, n_heads, D)

    nc = T // CMP_STRIDE
    cmp_end = jnp.arange(nc, dtype=jnp.int32) * CMP_STRIDE + (CMP_BLOCK - 1)
    cos_c, sin_c = _rope_tables(cmp_end.astype(F32))
    k_cmp, v_cmp = _nsa_compress(kc_in, vc_in, cmp_pos, cmp_w1, cmp_w2, cos_c, sin_c)

    n_sel = T // SEL_BLOCK
    c_start = jnp.arange(nc)[None, :] * CMP_STRIDE
    s_start = jnp.arange(n_sel)[:, None] * SEL_BLOCK
    ov = jnp.minimum(c_start + CMP_BLOCK, s_start + SEL_BLOCK) - jnp.maximum(c_start, s_start)
    overlap_t = (jnp.clip(ov, 0, None).astype(F32) / CMP_BLOCK).astype(BF16)

    o = _nsa_attention(q, k_cmp, v_cmp, k_sel, v_sel, k_win, v_win,
                       gate_logits.reshape(B, T, G * LANES), overlap_t)
    return _matmul_residual(o.reshape(B * T, D), w_out_bf, h_res, "nsa_out_proj", layer)


def _conv_silu_norm(x_ref, halo_ref, w_ref, xs_scr, o_scr, first, mode):
    tm = x_ref.shape[1]
    width = x_ref.shape[2]
    hr = CONV_HALO_ROWS
    xs_scr[0:hr, :] = jnp.where(first, 0.0, halo_ref[0])
    xs_scr[hr:hr + tm, :] = x_ref[0]
    xs = xs_scr[...]
    acc = xs[hr:] * w_ref[CONV_K - 1:CONV_K, :]
    for j in range(CONV_K - 1):
        back = CONV_K - 1 - j
        acc = acc + pltpu.roll(xs, back, 0)[hr:] * w_ref[j:j + 1, :]
    y = _silu(acc)
    if mode == "v":
        o_scr[...] = y
        return
    scale = HEAD_DIM ** -0.5 if mode == "q" else 1.0
    for h in range(width // HEAD_DIM):
        yh = y[:, h * HEAD_DIM:(h + 1) * HEAD_DIM]
        ss = jnp.sum(yh * yh, axis=-1, keepdims=True)
        o_scr[:, h * HEAD_DIM:(h + 1) * HEAD_DIM] = yh * lax.rsqrt(ss + NORM_EPS) * scale


def _gdn_gate_kernel(ab_ref, alog_ref, dtb_ref, g_ref, beta_ref, *, n_heads):
    a = ab_ref[:, :n_heads]
    b = ab_ref[:, n_heads:2 * n_heads]
    x = a + dtb_ref[...]
    softplus = jnp.maximum(x, 0.0) + jnp.log(1.0 + jnp.exp(-jnp.abs(x)))
    g_ref[...] = -jnp.exp(alog_ref[...]) * softplus
    beta_ref[...] = _sigmoid(b)


def _gdn_gates(ab, a_log, dt_bias, n_heads):
    n = ab.shape[0]
    tm = _pick(n, (1024, 512, 256, 128))
    row = pl.BlockSpec((tm, n_heads), lambda i: (i, 0))
    vec = pl.BlockSpec((1, n_heads), lambda i: (0, 0))
    return pl.pallas_call(
        functools.partial(_gdn_gate_kernel, n_heads=n_heads),
        grid=(n // tm,),
        in_specs=[pl.BlockSpec((tm, ab.shape[1]), lambda i: (i, 0)), vec, vec],
        out_specs=[row, row],
        out_shape=[jax.ShapeDtypeStruct((n, n_heads), F32)] * 2,
        compiler_params=_cparams(("parallel",)),
        name="gdn_gates",
    )(ab, a_log.reshape(1, n_heads), dt_bias.reshape(1, n_heads))


def _gdn_delta_kernel(xq_ref, xk_ref, xv_ref, hq_ref, hk_ref, hv_ref, wq_ref, wk_ref, wv_ref,
                      z_ref, g_ref, beta_ref, e_ref, nw_ref, o_ref,
                      s_scr, xs_scr, q_scr, k_scr, v_scr, *, hb, nc):
    C, d = GDN_CHUNK, HEAD_DIM
    rows_all = nc * C

    first = pl.program_id(2) == 0

    @pl.when(first)
    def _():
        s_scr[...] = jnp.zeros(s_scr.shape, F32)

    _conv_silu_norm(xq_ref, hq_ref, wq_ref, xs_scr.at[0], q_scr, first, "q")
    _conv_silu_norm(xk_ref, hk_ref, wk_ref, xs_scr.at[1], k_scr, first, "k")
    _conv_silu_norm(xv_ref, hv_ref, wv_ref, xs_scr.at[2], v_scr, first, "v")

    row = lax.broadcasted_iota(jnp.int32, (C, C), 0)
    col = lax.broadcasted_iota(jnp.int32, (C, C), 1)
    tri = row >= col
    strict = row > col
    eye = row == col
    eye_f = jnp.where(eye, 1.0, 0.0)
    r2 = lax.broadcasted_iota(jnp.int32, (rows_all, rows_all), 0)
    c2 = lax.broadcasted_iota(jnp.int32, (rows_all, rows_all), 1)
    chunk_lower = jnp.where((r2 >= c2) & (r2 // C == c2 // C), 1.0, 0.0).astype(BF16)

    gc = _dot_exact_by_f32(chunk_lower, g_ref[0])
    gc_b = _dot_f32_by_exact(gc, e_ref[0])
    beta_b = _dot_f32_by_exact(beta_ref[0], e_ref[0])

    pairs = [(c, h) for c in range(nc) for h in range(hb)]
    loc = []
    for c, h in pairs:
        rs, cols = slice(c * C, (c + 1) * C), slice(h * d, (h + 1) * d)
        q, k, v = q_scr[rs, cols], k_scr[rs, cols], v_scr[rs, cols]
        gcb, bb = gc_b[rs, cols], beta_b[rs, cols]
        gc_row = jnp.sum(jnp.where(eye, gcb[:, :C], 0.0), axis=0, keepdims=True)
        decay = jnp.exp(jnp.where(tri, gcb[:, :C] - gc_row, -jnp.inf))
        kb = k.astype(BF16)
        a_mat = jnp.where(strict, bb[:, :C] * _dot_nt(kb, kb) * decay, 0.0)
        loc.append(dict(q=q, k=k, v=v, gcb=gcb, bb=bb, decay=decay, kb=kb, a=a_mat))

    t_inv = [eye_f - p["a"] for p in loc]
    pw = [_dot_bf(p["a"], p["a"]) for p in loc]
    n_sq = max(int(math.ceil(math.log2(C))) - 1, 0)
    for i in range(n_sq):
        t_inv = [t + _dot_bf(t, p) for t, p in zip(t_inv, pw)]
        if i + 1 < n_sq:
            pw = [_dot_bf(p, p) for p in pw]

    uw = []
    for p, t in zip(loc, t_inv):
        p["eg"] = jnp.exp(p["gcb"])
        rhs = jnp.concatenate([p["v"] * p["bb"], p["k"] * (p["bb"] * p["eg"])], axis=1)
        uw.append(_dot(t.astype(BF16), rhs.astype(BF16)))
    wu_b = [jnp.concatenate([x[:, d:], x[:, :d]], axis=1).astype(BF16) for x in uw]
    qk_wu = [_dot((_dot_nt(p["q"].astype(BF16), p["kb"]) * p["decay"]).astype(BF16), wu)
             for p, wu in zip(loc, wu_b)]
    kd_wu = []
    for p, wu in zip(loc, wu_b):
        p["gc_last"] = p["gcb"][C - 1:C, :]
        k_dec = p["k"] * jnp.exp(p["gc_last"] - p["gcb"])
        kd_wu.append(_dot(k_dec.T.astype(BF16), wu))

    states = [s_scr[h] for h in range(hb)]
    for c in range(nc):
        idx = [c * hb + h for h in range(hb)]
        lhs = [jnp.concatenate([kd_wu[i][:, :d], loc[i]["q"] * loc[i]["eg"] - qk_wu[i][:, :d]],
                               axis=0).astype(BF16) for i in idx]
        prod = [_dot(l, s.astype(BF16)) for l, s in zip(lhs, states)]
        for h, i in enumerate(idx):
            o = prod[h][d:] + qk_wu[i][:, d:]
            states[h] = states[h] * jnp.exp(loc[i]["gc_last"]) - prod[h][:d] + kd_wu[i][:, d:]
            ms = jnp.mean(o * o, axis=-1, keepdims=True)
            y = o * lax.rsqrt(ms + NORM_EPS) * nw_ref[...]
            rs, cols = slice(c * C, (c + 1) * C), slice(h * d, (h + 1) * d)
            o_ref[0, rs, cols] = (y * _silu(z_ref[0, rs, cols])).astype(o_ref.dtype)
    for h in range(hb):
        s_scr[h] = states[h]


def _gdn_delta(proj, conv_w, g, beta, norm_w, n_heads):
    B, T, _ = proj.shape
    d, C = HEAD_DIM, GDN_CHUNK
    kw = n_heads * d
    hb = _pick(n_heads, (GDN_HEADS_PER_STEP, 2, 1))
    nc = _pick(T // C, (GDN_CHUNKS_PER_STEP, 2, 1))
    rows, width, hr = nc * C, hb * d, CONV_HALO_ROWS
    n_hg = n_heads // hb
    expand = (jnp.arange(n_heads)[None, :, None]
              == (jnp.arange(n_hg)[:, None, None] * hb + jnp.arange(width)[None, None, :] // d))
    expand = expand.astype(BF16)

    def cols(part):
        return pl.BlockSpec((1, rows, width), lambda b, hg, c: (b, c, part * n_hg + hg))

    def halo(part):
        return pl.BlockSpec((1, hr, width),
                            lambda b, hg, c: (b, jnp.maximum(c * (rows // hr) - 1, 0), part * n_hg + hg))

    def taps(part):
        return pl.BlockSpec((CONV_K, width), lambda b, hg, c: (0, part * n_hg + hg))

    gate_spec = pl.BlockSpec((1, rows, n_heads), lambda b, hg, c: (b, c, 0))
    return pl.pallas_call(
        functools.partial(_gdn_delta_kernel, hb=hb, nc=nc),
        grid=(B, n_hg, T // rows),
        in_specs=[cols(0), cols(1), cols(2), halo(0), halo(1), halo(2), taps(0), taps(1), taps(2),
                  cols(3), gate_spec, gate_spec,
                  pl.BlockSpec((1, n_heads, width), lambda b, hg, c: (hg, 0, 0)),
                  pl.BlockSpec((1, d), lambda b, hg, c: (0, 0))],
        out_specs=pl.BlockSpec((1, rows, width), lambda b, hg, c: (b, c, hg)),
        out_shape=jax.ShapeDtypeStruct((B, T, kw), BF16),
        scratch_shapes=[pltpu.VMEM((hb, d, d), F32), pltpu.VMEM((3, hr + rows, width), F32),
                        pltpu.VMEM((rows, width), F32), pltpu.VMEM((rows, width), F32),
                        pltpu.VMEM((rows, width), F32)],
        compiler_params=_cparams(("parallel", "parallel", "arbitrary")),
        name="gdn_delta",
    )(proj, proj, proj, proj, proj, proj, conv_w, conv_w, conv_w, proj,
      g.reshape(B, T, n_heads), beta.reshape(B, T, n_heads), expand, norm_w.reshape(1, d))


def _gdn_mixer(y, h_res, gain, w_in, w_in_bf, conv_w, a_log, dt_bias, norm_w, w_out_bf, layer, B, T):
    D = h_res.shape[1]
    n_heads = D // HEAD_DIM
    kw = n_heads * HEAD_DIM
    proj = _matmul(y, w_in_bf, F32, "gdn_in_proj", n=4 * kw, layer=layer).reshape(B, T, 4 * kw)
    w_ab = jnp.pad(gain[:, None] * w_in[:, 4 * kw:], ((0, 0), (0, LANES - 2 * n_heads)))
    ab = _matmul(y, w_ab.astype(BF16), F32, "gdn_ab_proj")
    g, beta = _gdn_gates(ab, a_log, dt_bias, n_heads)
    o = _gdn_delta(proj, conv_w, g, beta, norm_w, n_heads)
    return _matmul_residual(o.reshape(B * T, kw), w_out_bf, h_res, "gdn_out_proj", layer)


def kernel(x, norm_mix, norm_ffn, norm_final, nsa_w_in, nsa_cmp_pos, nsa_cmp_w1, nsa_cmp_w2,
           nsa_w_out, gdn_w_in, gdn_conv, gdn_a_log, gdn_dt_bias, gdn_norm, gdn_w_out,
           ffn_w_gate, ffn_w_up, ffn_w_down):
    B, T, D = x.shape
    depth = norm_mix.shape[0]
    h = x.reshape(B * T, D)
    bf = lambda w: w.astype(BF16)
    nsa_w_in_bf = bf(norm_mix[0::2][:, :, None] * nsa_w_in)
    gdn_w_in_bf = bf(norm_mix[1::2][:, :, None] * gdn_w_in)
    w_gate_bf, w_up_bf = bf(norm_ffn[:, :, None] * ffn_w_gate), bf(norm_ffn[:, :, None] * ffn_w_up)
    nsa_w_out_bf, gdn_w_out_bf, w_down_bf = bf(nsa_w_out), bf(gdn_w_out), bf(ffn_w_down)
    y = _stream_init(h)
    for layer in range(depth):
        i = layer // 2
        if layer % 2 == 0:
            h, y = _nsa_mixer(y, h, norm_mix[layer], nsa_w_in[i], nsa_w_in_bf, nsa_cmp_pos[i],
                              nsa_cmp_w1[i], nsa_cmp_w2[i], nsa_w_out_bf, i, B, T)
        else:
            h, y = _gdn_mixer(y, h, norm_mix[layer], gdn_w_in[i], gdn_w_in_bf, gdn_conv[i],
                              gdn_a_log[i], gdn_dt_bias[i], gdn_norm[i], gdn_w_out_bf, i, B, T)
        hid = _gateup(y, w_gate_bf, w_up_bf, layer)
        h, y = _matmul_residual(hid, w_down_bf, h, "ffn_down", layer)
    return _rmsnorm(h, norm_final, F32).reshape(B, T, D)
```

```python
import functools
import math

import jax
import jax.numpy as jnp
from jax import lax
from jax.experimental import pallas as pl
from jax.experimental.pallas import tpu as pltpu

F32 = jnp.float32
BF16 = jnp.bfloat16

HEAD_DIM = 128
ROPE_THETA = 10000.0
NORM_EPS = 1e-6
NSA_KV_GROUPS = 4
NSA_BRANCHES = 3
CMP_STRIDE = 16
CMP_BLOCK = 2 * CMP_STRIDE
SEL_BLOCK = 64
SEL_TOPK = 16
SEL_FORCED = 3
WINDOW = 512
NEG_INF = -1e30
CONV_K = 4
GDN_CHUNK = 64
LOG2_E = 1.4426950408889634

V7X_VMEM_LIMIT_BYTES = 56 * 1024 * 1024
LANES = 128
CONV_HALO_ROWS = 8
NSA_Q_TILE = 128
NSA_SEL_KEYS = 512
GDN_HEADS_PER_STEP = 8
GDN_CHUNKS_PER_STEP = 4


def _cparams(sem):
    return pltpu.CompilerParams(dimension_semantics=sem, vmem_limit_bytes=V7X_VMEM_LIMIT_BYTES)


def _pick(n, prefs):
    for p in prefs:
        if p <= n and n % p == 0:
            return p
    return n


def _dot(a, b):
    return jnp.dot(a, b, preferred_element_type=F32)


def _dot_nt(a, b):
    return lax.dot_general(a, b, (((1,), (1,)), ((), ())), preferred_element_type=F32)


def _split3(x):
    x1 = x.astype(BF16)
    r1 = x - x1.astype(F32)
    x2 = r1.astype(BF16)
    r2 = r1 - x2.astype(F32)
    return x1, x2, r2.astype(BF16)


def _dot_f32_by_exact(x, e):
    x1, x2, x3 = _split3(x)
    return _dot(x1, e) + _dot(x2, e) + _dot(x3, e)


def _dot_exact_by_f32(e, x):
    x1, x2, x3 = _split3(x)
    return _dot(e, x1) + _dot(e, x2) + _dot(e, x3)


def _dot_nt_exact_by_f32(e, x):
    x1, x2, x3 = _split3(x)
    return _dot_nt(e, x1) + _dot_nt(e, x2) + _dot_nt(e, x3)


def _dot_bf(a, b):
    return _dot(a.astype(BF16), b.astype(BF16))


def _sigmoid(x):
    return 1.0 / (1.0 + jnp.exp(-x))


def _silu(x):
    return x * _sigmoid(x)


def _rmsnorm_kernel(x_ref, w_ref, o_ref):
    x = x_ref[...]
    ms = jnp.mean(x * x, axis=-1, keepdims=True)
    o_ref[...] = (x * lax.rsqrt(ms + NORM_EPS) * w_ref[...]).astype(o_ref.dtype)


def _rmsnorm(x, w, out_dtype):
    n, d = x.shape
    tm = _pick(n, (256, 128, 64, 32, 16, 8))
    return pl.pallas_call(
        _rmsnorm_kernel,
        grid=(n // tm,),
        in_specs=[pl.BlockSpec((tm, d), lambda i: (i, 0)),
                  pl.BlockSpec((1, d), lambda i: (0, 0))],
        out_specs=pl.BlockSpec((tm, d), lambda i: (i, 0)),
        out_shape=jax.ShapeDtypeStruct((n, d), out_dtype),
        compiler_params=_cparams(("parallel",)),
        name="rmsnorm",
    )(x, w.reshape(1, d))


def _mm_kernel(x_ref, w_ref, o_ref):
    o_ref[...] = _dot(x_ref[...], w_ref[...]).astype(o_ref.dtype)


def _weight_spec(w, layer, block, index):
    if w.ndim == 2:
        return pl.BlockSpec(block, index)
    return pl.BlockSpec((None,) + block, lambda *g: (layer,) + index(*g))


def _matmul(x, w, out_dtype, name, n=None, layer=None):
    m, k = x.shape
    n = w.shape[-1] if n is None else n
    tm = _pick(m, (1024, 512, 256, 128))
    tn = _pick(n, (512, 256, 128))
    return pl.pallas_call(
        _mm_kernel,
        grid=(m // tm, n // tn),
        in_specs=[pl.BlockSpec((tm, k), lambda i, j: (i, 0)),
                  _weight_spec(w, layer, (k, tn), lambda i, j: (0, j))],
        out_specs=pl.BlockSpec((tm, tn), lambda i, j: (i, j)),
        out_shape=jax.ShapeDtypeStruct((m, n), out_dtype),
        compiler_params=_cparams(("parallel", "parallel")),
        name=name,
    )(x, w)


def _mm_res_kernel(x_ref, w_ref, r_ref, o_ref):
    @pl.when(pl.program_id(2) == 0)
    def _():
        o_ref[...] = r_ref[...]

    o_ref[...] += _dot(x_ref[...], w_ref[...])


def _matmul_residual(x, w, res, name, layer=None):
    m, k = x.shape
    n = w.shape[-1]
    tm = _pick(m, (1024, 512, 256, 128))
    tn = _pick(n, (512, 256, 128))
    tk = k if k <= 4096 else _pick(k, (5504, 4096, 2816, 2048, 1408, 1024, 512, 256, 128))
    return pl.pallas_call(
        _mm_res_kernel,
        grid=(m // tm, n // tn, k // tk),
        in_specs=[pl.BlockSpec((tm, tk), lambda i, j, l: (i, l)),
                  _weight_spec(w, layer, (tk, tn), lambda i, j, l: (l, j)),
                  pl.BlockSpec((tm, tn), lambda i, j, l: (i, j))],
        out_specs=pl.BlockSpec((tm, tn), lambda i, j, l: (i, j)),
        out_shape=jax.ShapeDtypeStruct((m, n), F32),
        compiler_params=_cparams(("parallel", "parallel", "arbitrary")),
        name=name,
    )(x, w, res)


def _gateup_kernel(x_ref, wg_ref, wu_ref, o_ref):
    x = x_ref[...]
    g = _dot(x, wg_ref[...])
    u = _dot(x, wu_ref[...])
    o_ref[...] = (_silu(g) * u).astype(o_ref.dtype)


def _gateup(x, wg, wu, layer=None):
    m, k = x.shape
    n = wg.shape[-1]
    tm = _pick(m, (1024, 512, 256, 128))
    tn = _pick(n, (512, 256, 128))
    return pl.pallas_call(
        _gateup_kernel,
        grid=(m // tm, n // tn),
        in_specs=[pl.BlockSpec((tm, k), lambda i, j: (i, 0)),
                  _weight_spec(wg, layer, (k, tn), lambda i, j: (0, j)),
                  _weight_spec(wu, layer, (k, tn), lambda i, j: (0, j))],
        out_specs=pl.BlockSpec((tm, tn), lambda i, j: (i, j)),
        out_shape=jax.ShapeDtypeStruct((m, n), BF16),
        compiler_params=_cparams(("parallel", "parallel")),
        name="ffn_gateup",
    )(x, wg, wu)


def _rope_tables(pos):
    inv = ROPE_THETA ** (-jnp.arange(0, HEAD_DIM, 2, dtype=F32) / HEAD_DIM)
    ang = pos[:, None] * inv[None, :]
    c, s = jnp.cos(ang), jnp.sin(ang)
    return jnp.concatenate([c, c], axis=-1), jnp.concatenate([-s, s], axis=-1)


def _rope(x, cos2, sin_signed):
    half = HEAD_DIM // 2
    rot = jnp.concatenate([x[:, half:], x[:, :half]], axis=-1)
    return x * cos2 + rot * sin_signed


def _nsa_prep_kernel(p_ref, cos_ref, sin_ref, q_ref, ks_ref, vs_ref, kw_ref, vw_ref,
                     kc_ref, vc_ref, *, n_heads, d_model):
    G, d = NSA_KV_GROUPS, HEAD_DIM
    cos2 = cos_ref[...]
    sin_s = sin_ref[...]
    scale = d ** -0.5 * LOG2_E
    for h in range(n_heads):
        x = p_ref[0, :, h * d:(h + 1) * d]
        q_ref[0, h] = (_rope(x, cos2, sin_s) * scale).astype(q_ref.dtype)

    def kv_cols(branch, kv, g):
        c0 = d_model + ((branch * 2 + kv) * G + g) * d
        return p_ref[0, :, c0:c0 + d]

    for g in range(G):
        kc_ref[0, g] = kv_cols(0, 0, g)
        vc_ref[0, g] = kv_cols(0, 1, g)
        ks_ref[0, g] = _rope(kv_cols(1, 0, g), cos2, sin_s).astype(ks_ref.dtype)
        vs_ref[0, g] = kv_cols(1, 1, g).astype(vs_ref.dtype)
        kw_ref[0, g] = _rope(kv_cols(2, 0, g), cos2, sin_s).astype(kw_ref.dtype)
        vw_ref[0, g] = kv_cols(2, 1, g).astype(vw_ref.dtype)


def _nsa_prep(proj, cos2, sin_s, n_heads, d_model):
    B, T, W = proj.shape
    G, d = NSA_KV_GROUPS, HEAD_DIM
    tm = _pick(T, (256, 128, 64))
    kv_spec = pl.BlockSpec((1, G, tm, d), lambda b, i: (b, 0, i, 0))
    kv_bf = jax.ShapeDtypeStruct((B, G, T, d), BF16)
    kv_f32 = jax.ShapeDtypeStruct((B, G, T, d), F32)
    return pl.pallas_call(
        functools.partial(_nsa_prep_kernel, n_heads=n_heads, d_model=d_model),
        grid=(B, T // tm),
        in_specs=[pl.BlockSpec((1, tm, W), lambda b, i: (b, i, 0)),
                  pl.BlockSpec((tm, d), lambda b, i: (i, 0)),
                  pl.BlockSpec((tm, d), lambda b, i: (i, 0))],
        out_specs=[pl.BlockSpec((1, n_heads, tm, d), lambda b, i: (b, 0, i, 0)),
                   kv_spec, kv_spec, kv_spec, kv_spec, kv_spec, kv_spec],
        out_shape=[jax.ShapeDtypeStruct((B, n_heads, T, d), BF16),
                   kv_bf, kv_bf, kv_bf, kv_bf, kv_f32, kv_f32],
        compiler_params=_cparams(("parallel", "parallel")),
        name="nsa_prep",
    )(proj, cos2, sin_s)


def _nsa_compress_kernel(ck_ref, cv_ref, pos_ref, w1_ref, w2_ref, cos_ref, sin_ref,
                         ko_ref, vo_ref):
    nc = ck_ref.shape[2]
    half = ck_ref.shape[3]
    for kv, (c_ref, o_ref) in enumerate(((ck_ref, ko_ref), (cv_ref, vo_ref))):
        c = c_ref[0, 0]
        pos = pos_ref[kv]
        a = _dot((c + pos[:, :half]).astype(BF16), w1_ref[kv, :half, :])
        b = _dot((c + pos[:, half:]).astype(BF16), w1_ref[kv, half:, :])
        hid = a + pltpu.roll(b, nc - 1, 0)
        y = _dot(jax.nn.gelu(hid, approximate=True).astype(BF16), w2_ref[kv])
        if kv == 0:
            y = _rope(y, cos_ref[...], sin_ref[...])
        o_ref[0, 0] = y.astype(o_ref.dtype)


def _nsa_compress(kc_in, vc_in, pos, w1, w2, cos_c, sin_c):
    B, G, T, d = kc_in.shape
    nc = T // CMP_STRIDE
    half = CMP_STRIDE * d
    ck = kc_in.reshape(B, G, nc, half)
    cv = vc_in.reshape(B, G, nc, half)
    hidden = w1.shape[-1]
    c_spec = pl.BlockSpec((1, 1, nc, half), lambda b, g: (b, g, 0, 0))
    o_spec = pl.BlockSpec((1, 1, nc, d), lambda b, g: (b, g, 0, 0))
    full = lambda shape: pl.BlockSpec(shape, lambda b, g: (0,) * len(shape))
    return pl.pallas_call(
        _nsa_compress_kernel,
        grid=(B, G),
        in_specs=[c_spec, c_spec, full((2, 1, 2 * half)), full((2, 2 * half, hidden)),
                  full((2, hidden, d)), full((nc, d)), full((nc, d))],
        out_specs=[o_spec, o_spec],
        out_shape=[jax.ShapeDtypeStruct((B, G, nc, d), BF16)] * 2,
        compiler_params=_cparams(("parallel", "parallel")),
        name="nsa_compress",
    )(ck, cv, pos.reshape(2, 1, 2 * half), w1.astype(BF16), w2.astype(BF16), cos_c, sin_c)


def _masked_keys(k, valid_t):
    return jnp.concatenate([k, jnp.where(valid_t, 0.0, NEG_INF).astype(BF16)], axis=1)


def _values_with_ones(v):
    return jnp.concatenate([v, jnp.ones(v.shape, BF16)], axis=1)


def _softmax_attend(q_aug, k_aug, v1):
    d = v1.shape[1] // 2
    s = _dot_nt(q_aug, k_aug)
    p = jnp.exp2((s - jnp.max(s, axis=1, keepdims=True)).astype(BF16))
    pv = _dot(p, v1)
    return pv[:, :d] / pv[:, d:]


def _nsa_attn_kernel(q_ref, kc_ref, vc_ref, ks_ref, vs_ref, kw_ref, vw_ref, gl_ref, ovt_ref,
                     o_ref, m_scr, acc_scr, ow_scr, sa_scr, sb_scr, *, hg, n_topk):
    tq = q_ref.shape[2]
    d = q_ref.shape[3]
    ncp = kc_ref.shape[2]
    ns = ovt_ref.shape[0]
    T = ks_ref.shape[2]
    R = hg * tq
    t0 = pl.program_id(2) * tq

    onehot = jnp.where(lax.broadcasted_iota(jnp.int32, (tq, tq), 0)
                       == lax.broadcasted_iota(jnp.int32, (tq, tq), 1), 1.0, 0.0).astype(BF16)
    q_aug = jnp.concatenate([q_ref[0].reshape(R, d), jnp.concatenate([onehot] * hg, axis=0)], axis=1)
    t_row = t0 + lax.broadcasted_iota(jnp.int32, (1, tq), 1)

    wk = min(WINDOW + tq, T)
    w0 = pl.multiple_of(jnp.maximum(t0 - WINDOW, 0), tq)
    kp = w0 + lax.broadcasted_iota(jnp.int32, (wk, tq), 0)
    kw_aug = _masked_keys(kw_ref[0, 0, pl.ds(w0, wk), :], (kp <= t_row) & (kp > t_row - WINDOW))
    vw = _values_with_ones(vw_ref[0, 0, pl.ds(w0, wk), :])
    ow_scr[...] = _softmax_attend(q_aug, kw_aug, vw)

    cmp_end = lax.broadcasted_iota(jnp.int32, (ncp, tq), 0) * CMP_STRIDE + (CMP_BLOCK - 1)
    kc_aug = _masked_keys(kc_ref[0, 0], cmp_end <= t_row)
    s = _dot_nt(q_aug, kc_aug)
    p = jnp.exp2(s - jnp.max(s, axis=1, keepdims=True))
    t_of_row = t0 + lax.broadcasted_iota(jnp.int32, (R, 1), 0) % tq
    inv_l = jnp.where(t_of_row >= CMP_BLOCK - 1, 1.0 / jnp.sum(p, axis=1, keepdims=True), 0.0)
    pc = p * inv_l
    o_c = _dot(pc.astype(BF16), vc_ref[0, 0])

    imp_t = _dot_nt_exact_by_f32(ovt_ref[...], jnp.sum(pc.reshape(hg, tq, ncp), axis=0))
    blk = lax.broadcasted_iota(jnp.int32, (ns, tq), 0)
    cur = t_row // SEL_BLOCK
    forced = (blk == 0) | (blk == cur) | (blk == cur - 1)
    x = jnp.where(forced, -jnp.inf, jnp.where(blk <= cur, imp_t, NEG_INF))
    blk_f = blk.astype(F32)
    sel = jnp.where(forced, 1.0, 0.0)
    for _ in range(n_topk - SEL_FORCED):
        mx = jnp.max(x, axis=0, keepdims=True)
        idx = jnp.min(jnp.where(x == mx, blk_f, float(ns)), axis=0, keepdims=True)
        hit = blk_f == idx
        sel = jnp.where(hit, 1.0, sel)
        x = jnp.where(hit, -jnp.inf, x)
    sel_b = sel.astype(BF16)

    cs = min(NSA_SEL_KEYS, T)
    m_scr[...] = jnp.full(m_scr.shape, NEG_INF, F32)
    acc_scr[...] = jnp.zeros(acc_scr.shape, F32)
    key_blk = lax.broadcasted_iota(jnp.int32, (cs, ns), 0) // SEL_BLOCK
    blk_col = lax.broadcasted_iota(jnp.int32, (cs, ns), 1)
    key_off = lax.broadcasted_iota(jnp.int32, (cs, tq), 0)

    n_chunks = (t0 + tq + cs - 1) // cs
    last_chunk = T // cs - 1

    def scores(c):
        k0 = pl.multiple_of(c * cs, cs)
        expand = jnp.where(key_blk + k0 // SEL_BLOCK == blk_col, 1.0, 0.0).astype(BF16)
        member = _dot(expand, sel_b)
        valid = (member > 0.5) & (key_off + k0 <= t_row)
        return _dot_nt(q_aug, _masked_keys(ks_ref[0, 0, pl.ds(k0, cs), :], valid))

    def attend(c, s_cur, s_next):
        if s_next is not None:
            s_next[...] = scores(jnp.minimum(c + 1, last_chunk))
        v1 = _values_with_ones(vs_ref[0, 0, pl.ds(pl.multiple_of(c * cs, cs), cs), :])
        s = s_cur[...]
        m_prev = m_scr[...]
        m_new = jnp.maximum(m_prev, jnp.max(s, axis=1, keepdims=True))
        alpha = jnp.exp2(m_prev - m_new)
        p = jnp.exp2((s - jnp.concatenate([m_new] * (cs // LANES), axis=1)).astype(BF16))
        acc_scr[...] = jnp.concatenate([alpha, alpha], axis=1) * acc_scr[...] + _dot(p, v1)
        m_scr[...] = m_new

    sa_scr[...] = scores(0)

    def pair(c):
        attend(c, sa_scr, sb_scr)
        attend(c + 1, sb_scr, sa_scr)

    def sel_body(j, carry):
        pair(4 * j)
        pair(4 * j + 2)
        return carry

    lax.fori_loop(0, n_chunks // 4, sel_body, 0)
    done = (n_chunks // 4) * 4

    @pl.when(n_chunks - done >= 2)
    def _():
        pair(done)

    @pl.when(n_chunks % 2 == 1)
    def _():
        attend(n_chunks - 1, sa_scr, None)

    o_s = acc_scr[:, :d] / acc_scr[:, d:]

    o_w = ow_scr[...]

    gate = _sigmoid(gl_ref[0])
    for h in range(hg):
        rows = slice(h * tq, (h + 1) * tq)
        out = (gate[:, h:h + 1] * o_c[rows]
               + gate[:, hg + h:hg + h + 1] * o_s[rows]
               + gate[:, 2 * hg + h:2 * hg + h + 1] * o_w[rows])
        o_ref[0, :, h * d:(h + 1) * d] = out.astype(o_ref.dtype)


def _nsa_attention(q, k_cmp, v_cmp, k_sel, v_sel, k_win, v_win, gate_logits, overlap_t):
    B, n_heads, T, d = q.shape
    G = NSA_KV_GROUPS
    hg = n_heads // G
    ncp = k_cmp.shape[2]
    ns = overlap_t.shape[0]
    tq = _pick(T, (NSA_Q_TILE,))
    R = hg * tq
    assert min(SEL_TOPK, ns) >= SEL_FORCED
    kv_spec = pl.BlockSpec((1, 1, T, d), lambda b, g, i: (b, g, 0, 0))
    cmp_spec = pl.BlockSpec((1, 1, ncp, d), lambda b, g, i: (b, g, 0, 0))
    return pl.pallas_call(
        functools.partial(_nsa_attn_kernel, hg=hg, n_topk=min(SEL_TOPK, ns)),
        grid=(B, G, T // tq),
        in_specs=[pl.BlockSpec((1, hg, tq, d), lambda b, g, i: (b, g, i, 0)),
                  cmp_spec, cmp_spec, kv_spec, kv_spec, kv_spec, kv_spec,
                  pl.BlockSpec((1, tq, LANES), lambda b, g, i: (b, i, g)),
                  pl.BlockSpec((ns, ncp), lambda b, g, i: (0, 0))],
        out_specs=pl.BlockSpec((1, tq, hg * d), lambda b, g, i: (b, i, g)),
        out_shape=jax.ShapeDtypeStruct((B, T, n_heads * d), BF16),
        scratch_shapes=[pltpu.VMEM((R, LANES), F32), pltpu.VMEM((R, 2 * d), F32),
                        pltpu.VMEM((R, d), F32),
                        pltpu.VMEM((R, min(NSA_SEL_KEYS, T)), F32),
                        pltpu.VMEM((R, min(NSA_SEL_KEYS, T)), F32)],
        compiler_params=_cparams(("parallel", "parallel", "parallel")),
        name="nsa_attention",
    )(q, k_cmp, v_cmp, k_sel, v_sel, k_win, v_win, gate_logits, overlap_t)


def _nsa_mixer(y, h_res, w_in, w_in_bf, cmp_pos, cmp_w1, cmp_w2, w_out_bf, layer, B, T):
    D = y.shape[1]
    d, G = HEAD_DIM, NSA_KV_GROUPS
    n_heads = D // d
    hg = n_heads // G
    kv_w = NSA_BRANCHES * 2 * G * d

    proj = _matmul(y, w_in_bf, F32, "nsa_in_proj", n=D + kv_w, layer=layer)
    wg = w_in[:, D + kv_w:].reshape(D, NSA_BRANCHES, G, hg).transpose(0, 2, 1, 3)
    wg = jnp.pad(wg.reshape(D, G, NSA_BRANCHES * hg), ((0, 0), (0, 0), (0, LANES - NSA_BRANCHES * hg)))
    gate_logits = _matmul(y, wg.reshape(D, G * LANES).astype(BF16), F32, "nsa_gate_proj")

    cos2, sin_s = _rope_tables(jnp.arange(T, dtype=F32))
    q, k_sel, v_sel, k_win, v_win, kc_in, vc_in = _nsa_prep(
        proj.reshape(B, T, D + kv_w), cos2, sin_s, n_heads, D)

    nc = T // CMP_STRIDE
    cmp_end = jnp.arange(nc, dtype=jnp.int32) * CMP_STRIDE + (CMP_BLOCK - 1)
    cos_c, sin_c = _rope_tables(cmp_end.astype(F32))
    k_cmp, v_cmp = _nsa_compress(kc_in, vc_in, cmp_pos, cmp_w1, cmp_w2, cos_c, sin_c)

    n_sel = T // SEL_BLOCK
    c_start = jnp.arange(nc)[None, :] * CMP_STRIDE
    s_start = jnp.arange(n_sel)[:, None] * SEL_BLOCK
    ov = jnp.minimum(c_start + CMP_BLOCK, s_start + SEL_BLOCK) - jnp.maximum(c_start, s_start)
    overlap_t = (jnp.clip(ov, 0, None).astype(F32) / CMP_BLOCK).astype(BF16)

    o = _nsa_attention(q, k_cmp, v_cmp, k_sel, v_sel, k_win, v_win,
                       gate_logits.reshape(B, T, G * LANES), overlap_t)
    return _matmul_residual(o.reshape(B * T, D), w_out_bf, h_res, "nsa_out_proj", layer)


def _conv_silu_norm(x_ref, halo_ref, w_ref, xs_scr, o_scr, first, mode):
    tm = x_ref.shape[1]
    width = x_ref.shape[2]
    hr = CONV_HALO_ROWS
    xs_scr[0:hr, :] = jnp.where(first, 0.0, halo_ref[0])
    xs_scr[hr:hr + tm, :] = x_ref[0]
    xs = xs_scr[...]
    acc = xs[hr:] * w_ref[CONV_K - 1:CONV_K, :]
    for j in range(CONV_K - 1):
        back = CONV_K - 1 - j
        acc = acc + pltpu.roll(xs, back, 0)[hr:] * w_ref[j:j + 1, :]
    y = _silu(acc)
    if mode == "v":
        o_scr[...] = y
        return
    scale = HEAD_DIM ** -0.5 if mode == "q" else 1.0
    for h in range(width // HEAD_DIM):
        yh = y[:, h * HEAD_DIM:(h + 1) * HEAD_DIM]
        ss = jnp.sum(yh * yh, axis=-1, keepdims=True)
        o_scr[:, h * HEAD_DIM:(h + 1) * HEAD_DIM] = yh * lax.rsqrt(ss + NORM_EPS) * scale


def _gdn_gate_kernel(ab_ref, alog_ref, dtb_ref, g_ref, beta_ref, *, n_heads):
    a = ab_ref[:, :n_heads]
    b = ab_ref[:, n_heads:2 * n_heads]
    x = a + dtb_ref[...]
    softplus = jnp.maximum(x, 0.0) + jnp.log(1.0 + jnp.exp(-jnp.abs(x)))
    g_ref[...] = -jnp.exp(alog_ref[...]) * softplus
    beta_ref[...] = _sigmoid(b)


def _gdn_gates(ab, a_log, dt_bias, n_heads):
    n = ab.shape[0]
    tm = _pick(n, (1024, 512, 256, 128))
    row = pl.BlockSpec((tm, n_heads), lambda i: (i, 0))
    vec = pl.BlockSpec((1, n_heads), lambda i: (0, 0))
    return pl.pallas_call(
        functools.partial(_gdn_gate_kernel, n_heads=n_heads),
        grid=(n // tm,),
        in_specs=[pl.BlockSpec((tm, ab.shape[1]), lambda i: (i, 0)), vec, vec],
        out_specs=[row, row],
        out_shape=[jax.ShapeDtypeStruct((n, n_heads), F32)] * 2,
        compiler_params=_cparams(("parallel",)),
        name="gdn_gates",
    )(ab, a_log.reshape(1, n_heads), dt_bias.reshape(1, n_heads))


def _gdn_delta_kernel(xq_ref, xk_ref, xv_ref, hq_ref, hk_ref, hv_ref, wq_ref, wk_ref, wv_ref,
                      z_ref, g_ref, beta_ref, e_ref, nw_ref, o_ref,
                      s_scr, xs_scr, q_scr, k_scr, v_scr, *, hb, nc):
    C, d = GDN_CHUNK, HEAD_DIM
    rows_all = nc * C

    first = pl.program_id(2) == 0

    @pl.when(first)
    def _():
        s_scr[...] = jnp.zeros(s_scr.shape, F32)

    _conv_silu_norm(xq_ref, hq_ref, wq_ref, xs_scr.at[0], q_scr, first, "q")
    _conv_silu_norm(xk_ref, hk_ref, wk_ref, xs_scr.at[1], k_scr, first, "k")
    _conv_silu_norm(xv_ref, hv_ref, wv_ref, xs_scr.at[2], v_scr, first, "v")

    row = lax.broadcasted_iota(jnp.int32, (C, C), 0)
    col = lax.broadcasted_iota(jnp.int32, (C, C), 1)
    tri = row >= col
    strict = row > col
    eye = row == col
    eye_f = jnp.where(eye, 1.0, 0.0)
    r2 = lax.broadcasted_iota(jnp.int32, (rows_all, rows_all), 0)
    c2 = lax.broadcasted_iota(jnp.int32, (rows_all, rows_all), 1)
    chunk_lower = jnp.where((r2 >= c2) & (r2 // C == c2 // C), 1.0, 0.0).astype(BF16)

    gc = _dot_exact_by_f32(chunk_lower, g_ref[0])
    gc_b = _dot_f32_by_exact(gc, e_ref[0])
    beta_b = _dot_f32_by_exact(beta_ref[0], e_ref[0])

    pairs = [(c, h) for c in range(nc) for h in range(hb)]
    loc = []
    for c, h in pairs:
        rs, cols = slice(c * C, (c + 1) * C), slice(h * d, (h + 1) * d)
        q, k, v = q_scr[rs, cols], k_scr[rs, cols], v_scr[rs, cols]
        gcb, bb = gc_b[rs, cols], beta_b[rs, cols]
        gc_row = jnp.sum(jnp.where(eye, gcb[:, :C], 0.0), axis=0, keepdims=True)
        decay = jnp.exp(jnp.where(tri, gcb[:, :C] - gc_row, -jnp.inf))
        kb = k.astype(BF16)
        a_mat = jnp.where(strict, bb[:, :C] * _dot_nt(kb, kb) * decay, 0.0)
        loc.append(dict(q=q, k=k, v=v, gcb=gcb, bb=bb, decay=decay, kb=kb, a=a_mat))

    t_inv = [eye_f - p["a"] for p in loc]
    pw = [_dot_bf(p["a"], p["a"]) for p in loc]
    n_sq = max(int(math.ceil(math.log2(C))) - 1, 0)
    for i in range(n_sq):
        t_inv = [t + _dot_bf(t, p) for t, p in zip(t_inv, pw)]
        if i + 1 < n_sq:
            pw = [_dot_bf(p, p) for p in pw]

    uw = []
    for p, t in zip(loc, t_inv):
        p["eg"] = jnp.exp(p["gcb"])
        rhs = jnp.concatenate([p["v"] * p["bb"], p["k"] * (p["bb"] * p["eg"])], axis=1)
        uw.append(_dot(t.astype(BF16), rhs.astype(BF16)))
    wu_b = [jnp.concatenate([x[:, d:], x[:, :d]], axis=1).astype(BF16) for x in uw]
    qk_wu = [_dot((_dot_nt(p["q"].astype(BF16), p["kb"]) * p["decay"]).astype(BF16), wu)
             for p, wu in zip(loc, wu_b)]
    kd_wu = []
    for p, wu in zip(loc, wu_b):
        p["gc_last"] = p["gcb"][C - 1:C, :]
        k_dec = p["k"] * jnp.exp(p["gc_last"] - p["gcb"])
        kd_wu.append(_dot(k_dec.T.astype(BF16), wu))

    states = [s_scr[h] for h in range(hb)]
    for c in range(nc):
        idx = [c * hb + h for h in range(hb)]
        lhs = [jnp.concatenate([kd_wu[i][:, :d], loc[i]["q"] * loc[i]["eg"] - qk_wu[i][:, :d]],
                               axis=0).astype(BF16) for i in idx]
        prod = [_dot(l, s.astype(BF16)) for l, s in zip(lhs, states)]
        for h, i in enumerate(idx):
            o = prod[h][d:] + qk_wu[i][:, d:]
            states[h] = states[h] * jnp.exp(loc[i]["gc_last"]) - prod[h][:d] + kd_wu[i][:, d:]
            ms = jnp.mean(o * o, axis=-1, keepdims=True)
            y = o * lax.rsqrt(ms + NORM_EPS) * nw_ref[...]
            rs, cols = slice(c * C, (c + 1) * C), slice(h * d, (h + 1) * d)
            o_ref[0, rs, cols] = (y * _silu(z_ref[0, rs, cols])).astype(o_ref.dtype)
    for h in range(hb):
        s_scr[h] = states[h]


def _gdn_delta(proj, conv_w, g, beta, norm_w, n_heads):
    B, T, _ = proj.shape
    d, C = HEAD_DIM, GDN_CHUNK
    kw = n_heads * d
    hb = _pick(n_heads, (GDN_HEADS_PER_STEP, 2, 1))
    nc = _pick(T // C, (GDN_CHUNKS_PER_STEP, 2, 1))
    rows, width, hr = nc * C, hb * d, CONV_HALO_ROWS
    n_hg = n_heads // hb
    expand = (jnp.arange(n_heads)[None, :, None]
              == (jnp.arange(n_hg)[:, None, None] * hb + jnp.arange(width)[None, None, :] // d))
    expand = expand.astype(BF16)

    def cols(part):
        return pl.BlockSpec((1, rows, width), lambda b, hg, c: (b, c, part * n_hg + hg))

    def halo(part):
        return pl.BlockSpec((1, hr, width),
                            lambda b, hg, c: (b, jnp.maximum(c * (rows // hr) - 1, 0), part * n_hg + hg))

    def taps(part):
        return pl.BlockSpec((CONV_K, width), lambda b, hg, c: (0, part * n_hg + hg))

    gate_spec = pl.BlockSpec((1, rows, n_heads), lambda b, hg, c: (b, c, 0))
    return pl.pallas_call(
        functools.partial(_gdn_delta_kernel, hb=hb, nc=nc),
        grid=(B, n_hg, T // rows),
        in_specs=[cols(0), cols(1), cols(2), halo(0), halo(1), halo(2), taps(0), taps(1), taps(2),
                  cols(3), gate_spec, gate_spec,
                  pl.BlockSpec((1, n_heads, width), lambda b, hg, c: (hg, 0, 0)),
                  pl.BlockSpec((1, d), lambda b, hg, c: (0, 0))],
        out_specs=pl.BlockSpec((1, rows, width), lambda b, hg, c: (b, c, hg)),
        out_shape=jax.ShapeDtypeStruct((B, T, kw), BF16),
        scratch_shapes=[pltpu.VMEM((hb, d, d), F32), pltpu.VMEM((3, hr + rows, width), F32),
                        pltpu.VMEM((rows, width), F32), pltpu.VMEM((rows, width), F32),
                        pltpu.VMEM((rows, width), F32)],
        compiler_params=_cparams(("parallel", "parallel", "arbitrary")),
        name="gdn_delta",
    )(proj, proj, proj, proj, proj, proj, conv_w, conv_w, conv_w, proj,
      g.reshape(B, T, n_heads), beta.reshape(B, T, n_heads), expand, norm_w.reshape(1, d))


def _gdn_mixer(y, h_res, w_in, w_in_bf, conv_w, a_log, dt_bias, norm_w, w_out_bf, layer, B, T):
    D = y.shape[1]
    n_heads = D // HEAD_DIM
    kw = n_heads * HEAD_DIM
    proj = _matmul(y, w_in_bf, F32, "gdn_in_proj", n=4 * kw, layer=layer).reshape(B, T, 4 * kw)
    w_ab = jnp.pad(w_in[:, 4 * kw:], ((0, 0), (0, LANES - 2 * n_heads)))
    ab = _matmul(y, w_ab.astype(BF16), F32, "gdn_ab_proj")
    g, beta = _gdn_gates(ab, a_log, dt_bias, n_heads)
    o = _gdn_delta(proj, conv_w, g, beta, norm_w, n_heads)
    return _matmul_residual(o.reshape(B * T, kw), w_out_bf, h_res, "gdn_out_proj", layer)


def kernel(x, norm_mix, norm_ffn, norm_final, nsa_w_in, nsa_cmp_pos, nsa_cmp_w1, nsa_cmp_w2,
           nsa_w_out, gdn_w_in, gdn_conv, gdn_a_log, gdn_dt_bias, gdn_norm, gdn_w_out,
           ffn_w_gate, ffn_w_up, ffn_w_down):
    B, T, D = x.shape
    depth = norm_mix.shape[0]
    h = x.reshape(B * T, D)
    bf = lambda w: w.astype(BF16)
    nsa_w_in_bf, nsa_w_out_bf = bf(nsa_w_in), bf(nsa_w_out)
    gdn_w_in_bf, gdn_w_out_bf = bf(gdn_w_in), bf(gdn_w_out)
    w_gate_bf, w_up_bf, w_down_bf = bf(ffn_w_gate), bf(ffn_w_up), bf(ffn_w_down)
    for layer in range(depth):
        y = _rmsnorm(h, norm_mix[layer], BF16)
        i = layer // 2
        if layer % 2 == 0:
            h = _nsa_mixer(y, h, nsa_w_in[i], nsa_w_in_bf, nsa_cmp_pos[i], nsa_cmp_w1[i],
                           nsa_cmp_w2[i], nsa_w_out_bf, i, B, T)
        else:
            h = _gdn_mixer(y, h, gdn_w_in[i], gdn_w_in_bf, gdn_conv[i], gdn_a_log[i],
                           gdn_dt_bias[i], gdn_norm[i], gdn_w_out_bf, i, B, T)
        y = _rmsnorm(h, norm_ffn[layer], BF16)
        hid = _gateup(y, w_gate_bf, w_up_bf, layer)
        h = _matmul_residual(hid, w_down_bf, h, "ffn_down", layer)
    return _rmsnorm(h, norm_final, F32).reshape(B, T, D)
```
